```python
import jax, jax.numpy as jnp
from jax import lax
import numpy as np

D_MODEL = 1024
BATCH = 2
SEQ = 16384
DEPTH = 1
DEC_BATCH = 4
DEC_SEQ = 8192
PAST_LEN = 128

N_MEM = 256
POOL_WINDOWS = (2, 4, 8, 16)
POOL_GROUPS = 4
POOL_GROUP_DIM = D_MODEL // 8
POOL_DIM = POOL_GROUPS * POOL_GROUP_DIM
GLA_HEADS = 4
GLA_DK = D_MODEL // 16
GLA_DV = D_MODEL // 8
GLA_QK = GLA_HEADS * GLA_DK
GLA_V = GLA_HEADS * GLA_DV
GLA_GATE_RANK = 16
GLA_GATE_TAU = 16.0
GLA_CHUNK = 64
MEM_HEADS = 4
MEM_HEAD_DIM = D_MODEL // 8
MEM_DIM = MEM_HEADS * MEM_HEAD_DIM
N_BRANCH = 3
N_EXPERTS = 16
EXPERT_DIM = D_MODEL
EC_CAPACITY_FACTOR = 2
EPS = 1e-6

IN_SPLITS = (POOL_DIM, GLA_QK, GLA_QK, GLA_V, GLA_V, GLA_GATE_RANK, GLA_GATE_RANK, MEM_DIM, N_BRANCH * D_MODEL)
IN_DIM = POOL_DIM + 2 * GLA_QK + 2 * GLA_V + 2 * GLA_GATE_RANK + MEM_DIM + N_BRANCH * D_MODEL

kernel_name = "hybrid_pool_gla_mem_ec_encoder"


def _offsets(sizes):
    out, acc = [], 0
    for s in sizes[:-1]:
        acc += s
        out.append(acc)
    return out


def rmsnorm(x, g):
    xf = x.astype(jnp.float32)
    y = xf * lax.rsqrt(jnp.mean(xf * xf, axis=-1, keepdims=True) + EPS)
    return (y * g).astype(x.dtype)


def pool_mixer(p, w_pool, scale):
    B, S, _ = p.shape
    pf = p.astype(jnp.float32)
    cs = jnp.concatenate([jnp.zeros((B, 1, POOL_DIM), jnp.float32), jnp.cumsum(pf, axis=1)], axis=1)
    pos = jnp.arange(S)
    outs = []
    for g, w in enumerate(POOL_WINDOWS):
        sl = slice(g * POOL_GROUP_DIM, (g + 1) * POOL_GROUP_DIM)
        lo = jnp.clip(pos - w // 2, 0, S)
        hi = jnp.clip(pos + (w - w // 2), 0, S)
        cnt = (hi - lo).astype(jnp.float32)[None, :, None]
        csg = cs[:, :, sl]
        mean = (jnp.take(csg, hi, axis=1) - jnp.take(csg, lo, axis=1)) / cnt
        outs.append(mean - pf[:, :, sl])
    d = jnp.stack(outs, axis=2).astype(p.dtype)
    y = jnp.einsum('bsgc,gcd->bsgd', d, w_pool).reshape(B, S, POOL_DIM)
    return (y * scale).astype(p.dtype)


def gla_direction(q, k, v, log_a):
    B, H, S, DK = q.shape
    DV = v.shape[-1]
    C = GLA_CHUNK
    N = S // C
    rs = lambda t: t.reshape(B, H, N, C, t.shape[-1])
    q, k, v, log_a = rs(q), rs(k), rs(v), rs(log_a)
    bcum = jnp.cumsum(log_a, axis=3)
    blast = bcum[:, :, :, -1:, :]
    q_s = q * jnp.exp(bcum)
    k_s = k * jnp.exp(-bcum)
    k_end = k * jnp.exp(blast - bcum)
    mask = jnp.tril(jnp.ones((C, C), dtype=bool))
    attn = jnp.where(mask, jnp.einsum('bhnik,bhnjk->bhnij', q_s, k_s), 0.0)
    o_intra = jnp.einsum('bhnij,bhnjv->bhniv', attn, v)
    kv = jnp.einsum('bhnck,bhncv->nbhkv', k_end, v)
    decay = jnp.exp(blast[:, :, :, 0, :]).transpose(2, 0, 1, 3)

    def step(state, inp):
        d, kv_n = inp
        return d[..., None] * state + kv_n, state

    _, s_prev = lax.scan(step, jnp.zeros((B, H, DK, DV), jnp.float32), (decay, kv))
    o_inter = jnp.einsum('bhnck,nbhkv->bhncv', q_s, s_prev)
    return (o_intra + o_inter).reshape(B, H, S, DV)


def gla_mixer(q, k, v, r, glr_f, glr_b, w2_f, b_f, w2_b, b_b, norm_g):
    B, S, _ = q.shape
    heads = lambda t, d: t.reshape(B, S, GLA_HEADS, d).transpose(0, 2, 1, 3).astype(jnp.float32)
    qh = heads(q, GLA_DK) * (GLA_DK ** -0.5)
    kh = heads(k, GLA_DK)
    vh = heads(v, GLA_DV)

    def log_decay(lr, w2, b):
        z = (lr @ w2 + b).astype(jnp.float32)
        return heads(jax.nn.log_sigmoid(z) / GLA_GATE_TAU, GLA_DK)

    flip = lambda t: jnp.flip(t, axis=2)
    o_f = gla_direction(qh, kh, vh, log_decay(glr_f, w2_f, b_f))
    o_b = flip(gla_direction(flip(qh), flip(kh), flip(vh), flip(log_decay(glr_b, w2_b, b_b))))
    o = o_f + o_b
    o = o * lax.rsqrt(jnp.mean(o * o, axis=-1, keepdims=True) + EPS)
    o = o.transpose(0, 2, 1, 3).reshape(B, S, GLA_V) * norm_g
    return (o * jax.nn.silu(r.astype(jnp.float32))).astype(q.dtype)


def memory_attention(qm, mem, mem_norm_g, w_mem_kv):
    B, S, _ = qm.shape
    M = mem.shape[1]
    kv = rmsnorm(mem, mem_norm_g) @ w_mem_kv
    km, vm = jnp.split(kv, 2, axis=-1)
    qh = qm.reshape(B, S, MEM_HEADS, MEM_HEAD_DIM)
    kh = km.reshape(B, M, MEM_HEADS, MEM_HEAD_DIM)
    vh = vm.reshape(B, M, MEM_HEADS, MEM_HEAD_DIM)
    s = jnp.einsum('bshd,bmhd->bhsm', qh, kh).astype(jnp.float32) * (MEM_HEAD_DIM ** -0.5)
    p = jax.nn.softmax(s, axis=-1).astype(vh.dtype)
    o = jnp.einsum('bhsm,bmhd->bshd', p, vh)
    return o.reshape(B, S, MEM_DIM)


def mixing_sublayer(h, mem, w_in, gla_w2_f, gla_b_f, gla_w2_b, gla_b_b, gla_norm_g, pool_w, pool_scale,
                    mem_norm_g, w_mem_kv, w_up_pool, w_up_gla, w_up_mem, w_out):
    B, S, D = h.shape
    pool_in, q, k, v, r, glr_f, glr_b, qm, gate_logits = jnp.split(h @ w_in, _offsets(IN_SPLITS), axis=-1)
    y_pool = pool_mixer(pool_in, pool_w, pool_scale) @ w_up_pool
    y_gla = gla_mixer(q, k, v, r, glr_f, glr_b, gla_w2_f, gla_b_f, gla_w2_b, gla_b_b, gla_norm_g) @ w_up_gla
    y_mem = memory_attention(qm, mem, mem_norm_g, w_mem_kv) @ w_up_mem
    gates = jax.nn.sigmoid(gate_logits.astype(jnp.float32)).reshape(B, S, N_BRANCH, D)
    merged = gates[:, :, 0] * y_pool + gates[:, :, 1] * y_gla + gates[:, :, 2] * y_mem
    return merged.astype(h.dtype) @ w_out


def expert_choice_ffn(h, w_router, w_e_gate, w_e_up, w_e_down):
    B, S, D = h.shape
    T = B * S
    xt = h.reshape(T, D)
    aff = jax.nn.softmax((xt @ w_router).astype(jnp.float32), axis=-1)
    cap = max(1, min(T, EC_CAPACITY_FACTOR * T // N_EXPERTS))
    g, idx = lax.top_k(aff.T, cap)
    xe = jnp.take(xt, idx, axis=0)
    hid = jax.nn.silu(jnp.einsum('ecd,edf->ecf', xe, w_e_gate)) * jnp.einsum('ecd,edf->ecf', xe, w_e_up)
    ye = jnp.einsum('ecf,efd->ecd', hid, w_e_down) * g[..., None].astype(xe.dtype)
    out = jnp.zeros_like(xt).at[idx.reshape(-1)].add(ye.reshape(-1, D))
    return out.reshape(B, S, D)


def trunk(x, mem, norm_mix_g, w_in, gla_w2_f, gla_b_f, gla_w2_b, gla_b_b, gla_norm_g, pool_w, pool_scale,
          mem_norm_g, w_mem_kv, w_up_pool, w_up_gla, w_up_mem, w_out, norm_ffn_g, w_router,
          w_e_gate, w_e_up, w_e_down, norm_final_g):
    for l in range(DEPTH):
        h = rmsnorm(x, norm_mix_g[l])
        x = x + mixing_sublayer(h, mem, w_in[l], gla_w2_f[l], gla_b_f[l], gla_w2_b[l], gla_b_b[l], gla_norm_g[l],
                                pool_w[l], pool_scale[l], mem_norm_g[l], w_mem_kv[l], w_up_pool[l], w_up_gla[l],
                                w_up_mem[l], w_out[l])
        h = rmsnorm(x, norm_ffn_g[l])
        x = x + expert_choice_ffn(h, w_router[l], w_e_gate[l], w_e_up[l], w_e_down[l])
    return rmsnorm(x, norm_final_g)


def setup_inputs(seed: int = 0) -> dict:
    key = jax.random.key(seed)
    ks = jax.random.split(key, 32)
    f32 = jnp.float32
    nrm = lambda k, shape, scale: (jax.random.normal(k, shape, f32) * scale)
    gain = lambda k, shape: 1.0 + 0.02 * jax.random.normal(k, shape, f32)
    L, D = DEPTH, D_MODEL
    return {
        "x_prompt": nrm(ks[0], (BATCH, SEQ, D), 1.0),
        "x_sample": nrm(ks[1], (DEC_BATCH, DEC_SEQ, D), 1.0),
        "mem_prompt": nrm(ks[2], (BATCH, N_MEM, D), 1.0),
        "mem_sample": nrm(ks[3], (DEC_BATCH, N_MEM, D), 1.0),
        "norm_mix_g": gain(ks[4], (L, D)),
        "w_in": nrm(ks[5], (L, D, IN_DIM), D ** -0.5),
        "gla_w2_f": nrm(ks[6], (L, GLA_GATE_RANK, GLA_QK), GLA_GATE_RANK ** -0.5),
        "gla_b_f": nrm(ks[7], (L, GLA_QK), 0.1),
        "gla_w2_b": nrm(ks[8], (L, GLA_GATE_RANK, GLA_QK), GLA_GATE_RANK ** -0.5),
        "gla_b_b": nrm(ks[9], (L, GLA_QK), 0.1),
        "gla_norm_g": gain(ks[10], (L, GLA_V)),
        "pool_w": nrm(ks[11], (L, POOL_GROUPS, POOL_GROUP_DIM, POOL_GROUP_DIM), POOL_GROUP_DIM ** -0.5),
        "pool_scale": gain(ks[12], (L, POOL_DIM)),
        "mem_norm_g": gain(ks[13], (L, D)),
        "w_mem_kv": nrm(ks[14], (L, D, 2 * MEM_DIM), D ** -0.5),
        "w_up_pool": nrm(ks[15], (L, POOL_DIM, D), POOL_DIM ** -0.5),
        "w_up_gla": nrm(ks[16], (L, GLA_V, D), GLA_V ** -0.5),
        "w_up_mem": nrm(ks[17], (L, MEM_DIM, D), MEM_DIM ** -0.5),
        "w_out": nrm(ks[18], (L, D, D), D ** -0.5),
        "norm_ffn_g": gain(ks[19], (L, D)),
        "w_router": nrm(ks[20], (L, D, N_EXPERTS), D ** -0.5),
        "w_e_gate": nrm(ks[21], (L, N_EXPERTS, D, EXPERT_DIM), D ** -0.5),
        "w_e_up": nrm(ks[22], (L, N_EXPERTS, D, EXPERT_DIM), D ** -0.5),
        "w_e_down": nrm(ks[23], (L, N_EXPERTS, EXPERT_DIM, D), EXPERT_DIM ** -0.5),
        "norm_final_g": gain(ks[24], (D,)),
    }


def reference(x_prompt, x_sample, mem_prompt, mem_sample, norm_mix_g, w_in, gla_w2_f, gla_b_f, gla_w2_b, gla_b_b,
              gla_norm_g, pool_w, pool_scale, mem_norm_g, w_mem_kv, w_up_pool, w_up_gla, w_up_mem, w_out,
              norm_ffn_g, w_router, w_e_gate, w_e_up, w_e_down, norm_final_g):
    y_prompt = trunk(x_prompt, mem_prompt, norm_mix_g, w_in, gla_w2_f, gla_b_f, gla_w2_b, gla_b_b, gla_norm_g,
                     pool_w, pool_scale, mem_norm_g, w_mem_kv, w_up_pool, w_up_gla, w_up_mem, w_out,
                     norm_ffn_g, w_router, w_e_gate, w_e_up, w_e_down, norm_final_g)
    y_sample = trunk(x_sample, mem_sample, norm_mix_g, w_in, gla_w2_f, gla_b_f, gla_w2_b, gla_b_b, gla_norm_g,
                     pool_w, pool_scale, mem_norm_g, w_mem_kv, w_up_pool, w_up_gla, w_up_mem, w_out,
                     norm_ffn_g, w_router, w_e_gate, w_e_up, w_e_down, norm_final_g)
    return (y_prompt, y_sample)
```

```python
import functools

import jax
import jax.numpy as jnp
from jax import lax
from jax.experimental import pallas as pl
from jax.experimental.pallas import tpu as pltpu

F32 = jnp.float32
BF16 = jnp.bfloat16
I32 = jnp.int32

D_MODEL = 1024
N_MEM = 256
POOL_WINDOWS = (2, 4, 8, 16)
POOL_GROUP_DIM = 128
POOL_DIM = 512
POOL_HALO = 8
GLA_HEADS = 4
GLA_DK = 64
GLA_DV = 128
GLA_QK = 256
GLA_V = 512
GLA_GATE_RANK = 16
GLA_GATE_TAU = 16.0
GLA_CHUNK = 64
MEM_HEADS = 4
MEM_HEAD_DIM = 128
MEM_DIM = 512
N_BRANCH = 3
N_EXPERTS = 16
EC_CAPACITY_FACTOR = 2
EPS = 1e-6

LANES = 128
MXU_DIM = 256

COL_GATE = 0
COL_POOL = COL_GATE + N_BRANCH * D_MODEL
COL_V = COL_POOL + POOL_DIM
COL_R = COL_V + GLA_V
COL_QM = COL_R + GLA_V
COL_Q = COL_QM + MEM_DIM
COL_K = COL_Q + GLA_QK
COL_LR = COL_K + GLA_QK
LR_WIDTH = MXU_DIM
PROJ_DIM = COL_LR + LR_WIDTH
PROJ_CHUNKS = ((0, 1536), (1536, 1536), (3072, 1536), (4608, 1280))

VMEM_LIMIT = 56 * 1024 * 1024


def _cparams(semantics):
    return pltpu.CompilerParams(dimension_semantics=semantics, vmem_limit_bytes=VMEM_LIMIT)


def _rms(x, g):
    return x * lax.rsqrt(jnp.mean(x * x, axis=-1, keepdims=True) + EPS) * g


def _split2(x):
    hi = x.astype(BF16)
    lo = (x - hi.astype(F32)).astype(BF16)
    return hi, lo


def _split3(x):
    hi = x.astype(BF16)
    r = x - hi.astype(F32)
    mid = r.astype(BF16)
    lo = (r - mid.astype(F32)).astype(BF16)
    return hi, mid, lo


def _dot(a, b):
    return jnp.dot(a, b, preferred_element_type=F32)


def _dot_nt(a, b):
    return lax.dot_general(a, b, (((1,), (1,)), ((), ())), preferred_element_type=F32)


def _dot_tn(a, b):
    return lax.dot_general(a, b, (((0,), (0,)), ((), ())), preferred_element_type=F32)


def _dot3(a_hi, a_lo, b_hi, b_lo):
    return _dot(a_hi, b_hi) + _dot(a_lo, b_hi) + _dot(a_hi, b_lo)


def _memkv_kernel(mem_ref, g_ref, w_ref, o_ref):
    h = _rms(mem_ref[0], g_ref[...]).astype(BF16)
    o_ref[0] = _dot(h, w_ref[...]).astype(BF16)


def _mem_kv(mem, g, w_kv):
    nb = mem.shape[0]
    return pl.pallas_call(
        _memkv_kernel,
        out_shape=jax.ShapeDtypeStruct((nb, N_MEM, 2 * MEM_DIM), BF16),
        grid=(nb,),
        in_specs=[
            pl.BlockSpec((1, N_MEM, D_MODEL), lambda b: (b, 0, 0)),
            pl.BlockSpec((1, D_MODEL), lambda b: (0, 0)),
            pl.BlockSpec((D_MODEL, 2 * MEM_DIM), lambda b: (0, 0)),
        ],
        out_specs=pl.BlockSpec((1, N_MEM, 2 * MEM_DIM), lambda b: (b, 0, 0)),
        compiler_params=_cparams(("arbitrary",)),
        name="mem_kv",
    )(mem, g, w_kv)


def _inproj_kernel(x_ref, g_ref, w_ref, o_ref):
    h = _rms(x_ref[...], g_ref[...]).astype(BF16)
    for c0, cw in PROJ_CHUNKS:
        o_ref[:, c0:c0 + cw] = _dot(h, w_ref[:, c0:c0 + cw])


def _inproj(x2, g, w_in_r, tm):
    t = x2.shape[0]
    return pl.pallas_call(
        _inproj_kernel,
        out_shape=jax.ShapeDtypeStruct((t, PROJ_DIM), F32),
        grid=(t // tm,),
        in_specs=[
            pl.BlockSpec((tm, D_MODEL), lambda i: (i, 0)),
            pl.BlockSpec((1, D_MODEL), lambda i: (0, 0)),
            pl.BlockSpec((D_MODEL, PROJ_DIM), lambda i: (0, 0)),
        ],
        out_specs=pl.BlockSpec((tm, PROJ_DIM), lambda i: (i, 0)),
        compiler_params=_cparams(("arbitrary",)),
        name="inproj",
    )(x2, g, w_in_r)


def _gla_chunk(q, k, v, lr, w2_hi, w2_lo, bias, st_ref, reverse):
    c = GLA_CHUNK
    lr_hi, lr_lo = _split2(lr)
    z = _dot3(lr_hi, lr_lo, w2_hi, w2_lo) + bias
    log_a = (jnp.minimum(z, 0.0) - jnp.log(1.0 + jnp.exp(-jnp.abs(z)))) * (1.0 / GLA_GATE_TAU)
    row = lax.broadcasted_iota(I32, (c, c), 0)
    col = lax.broadcasted_iota(I32, (c, c), 1)
    keep = (col >= row) if reverse else (col <= row)
    tri = jnp.where(keep, 1.0, 0.0).astype(BF16)
    la_hi, la_lo = _split2(log_a)
    bcum = _dot(tri, la_hi) + _dot(tri, la_lo)
    blast = bcum[0:1, :] if reverse else bcum[c - 1:c, :]
    q_s = (q * (jnp.exp(bcum) * (GLA_DK ** -0.5))).astype(BF16)
    k_s = (k * jnp.exp(-bcum)).astype(BF16)
    k_end = (k * jnp.exp(blast - bcum)).astype(BF16)
    decay = jnp.exp(blast)
    v16 = v.astype(BF16)
    outs = []
    for h in range(GLA_HEADS):
        ks = slice(h * GLA_DK, (h + 1) * GLA_DK)
        vs = slice(h * GLA_DV, (h + 1) * GLA_DV)
        st = st_ref[h]
        attn = jnp.where(keep, _dot_nt(q_s[:, ks], k_s[:, ks]), 0.0).astype(BF16)
        o_h = _dot(attn, v16[:, vs]) + _dot_nt(q_s[:, ks], st.astype(BF16))
        st_ref[h] = st * decay[:, ks] + _dot_tn(v16[:, vs], k_end[:, ks])
        outs.append(o_h)
    return jnp.concatenate(outs, axis=1)


def _gla_kernel(qf, kf, vf, lf, qb, kb, vb, lb, w2f_hi, w2f_lo, bf_ref, w2b_hi, w2b_lo, bb_ref,
                of_ref, ob_ref, sf_ref, sb_ref, *, nchunk):
    @pl.when(pl.program_id(1) == 0)
    def _():
        sf_ref[...] = jnp.zeros_like(sf_ref)
        sb_ref[...] = jnp.zeros_like(sb_ref)

    def body(c, carry):
        rf = pl.multiple_of(c * GLA_CHUNK, GLA_CHUNK)
        rb = pl.multiple_of((nchunk - 1 - c) * GLA_CHUNK, GLA_CHUNK)
        sl_f = pl.ds(rf, GLA_CHUNK)
        sl_b = pl.ds(rb, GLA_CHUNK)
        of_ref[sl_f, :] = _gla_chunk(qf[sl_f, :], kf[sl_f, :], vf[sl_f, :], lf[sl_f, :],
                                     w2f_hi[...], w2f_lo[...], bf_ref[...], sf_ref, False)
        ob_ref[sl_b, :] = _gla_chunk(qb[sl_b, :], kb[sl_b, :], vb[sl_b, :], lb[sl_b, :],
                                     w2b_hi[...], w2b_lo[...], bb_ref[...], sb_ref, True)
        return carry

    lax.fori_loop(0, nchunk, body, 0)


def _gla(proj, w2f, bf, w2b, bb, nbatch, seq, tb):
    t = proj.shape[0]
    ns = seq // tb
    w_q, w_v = GLA_QK, GLA_V

    def fwd(cb):
        return lambda b, i: (b * ns + i, cb)

    def bwd(cb):
        return lambda b, i: (b * ns + ns - 1 - i, cb)

    const = lambda b, i: (0, 0)
    in_specs = [
        pl.BlockSpec((tb, w_q), fwd(COL_Q // w_q)),
        pl.BlockSpec((tb, w_q), fwd(COL_K // w_q)),
        pl.BlockSpec((tb, w_v), fwd(COL_V // w_v)),
        pl.BlockSpec((tb, LR_WIDTH), fwd(COL_LR // LR_WIDTH)),
        pl.BlockSpec((tb, w_q), bwd(COL_Q // w_q)),
        pl.BlockSpec((tb, w_q), bwd(COL_K // w_q)),
        pl.BlockSpec((tb, w_v), bwd(COL_V // w_v)),
        pl.BlockSpec((tb, LR_WIDTH), bwd(COL_LR // LR_WIDTH)),
        pl.BlockSpec((LR_WIDTH, GLA_QK), const),
        pl.BlockSpec((LR_WIDTH, GLA_QK), const),
        pl.BlockSpec((1, GLA_QK), const),
        pl.BlockSpec((LR_WIDTH, GLA_QK), const),
        pl.BlockSpec((LR_WIDTH, GLA_QK), const),
        pl.BlockSpec((1, GLA_QK), const),
    ]
    out_specs = [
        pl.BlockSpec((tb, w_v), lambda b, i: (b * ns + i, 0)),
        pl.BlockSpec((tb, w_v), lambda b, i: (b * ns + ns - 1 - i, 0)),
    ]
    w2f_hi, w2f_lo = w2f
    w2b_hi, w2b_lo = w2b
    return pl.pallas_call(
        functools.partial(_gla_kernel, nchunk=tb // GLA_CHUNK),
        out_shape=[jax.ShapeDtypeStruct((t, w_v), F32), jax.ShapeDtypeStruct((t, w_v), F32)],
        grid=(nbatch, ns),
        in_specs=in_specs,
        out_specs=out_specs,
        scratch_shapes=[pltpu.VMEM((GLA_HEADS, GLA_DV, GLA_DK), F32),
                        pltpu.VMEM((GLA_HEADS, GLA_DV, GLA_DK), F32)],
        compiler_params=_cparams(("arbitrary", "arbitrary")),
        name="gla",
    )(proj, proj, proj, proj, proj, proj, proj, proj, w2f_hi, w2f_lo, bf, w2b_hi, w2b_lo, bb)


def _mix_kernel(x_ref, gate_ref, pool_ref, pprev_ref, pnext_ref, r_ref, qm_ref, of_ref, ob_ref,
                kv_ref, poolw_ref, pscale_ref, gnorm_ref, wup_pool_ref, wup_gla_ref, wup_mem_ref,
                wout_ref, gffn_ref, wr_hi_ref, wr_lo_ref,
                x1_ref, h2_ref, aff_ref, pp_ref, *, tm, seq):
    s0 = (pl.program_id(0) % (seq // tm)) * tm
    hal = POOL_HALO

    rows = lax.broadcasted_iota(I32, (hal, 1), 0)
    pp_ref[0:hal, :] = jnp.where(s0 - hal + rows >= 0, pprev_ref[...], 0.0)
    pp_ref[hal:hal + tm, :] = pool_ref[...]
    pp_ref[hal + tm:hal + tm + hal, :] = jnp.where(s0 + tm + rows < seq, pnext_ref[...], 0.0)
    pos = s0 + lax.broadcasted_iota(I32, (tm, 1), 0)
    ypool = []
    for g, w in enumerate(POOL_WINDOWS):
        cs = slice(g * POOL_GROUP_DIM, (g + 1) * POOL_GROUP_DIM)
        tot = pp_ref[hal - w // 2:hal - w // 2 + tm, cs]
        for off in range(-(w // 2) + 1, w - w // 2):
            tot = tot + pp_ref[hal + off:hal + off + tm, cs]
        cnt = (jnp.minimum(pos + (w - w // 2), seq) - jnp.maximum(pos - w // 2, 0)).astype(F32)
        dlt = (tot / cnt - pp_ref[hal:hal + tm, cs]).astype(BF16)
        ypool.append(_dot(dlt, poolw_ref[g]))
    y_pool = (jnp.concatenate(ypool, axis=1) * pscale_ref[...]).astype(BF16)
    up_pool = _dot(y_pool, wup_pool_ref[...])

    o = of_ref[...] + ob_ref[...]
    on = []
    for h in range(GLA_HEADS):
        oh = o[:, h * GLA_DV:(h + 1) * GLA_DV]
        on.append(oh * lax.rsqrt(jnp.mean(oh * oh, axis=-1, keepdims=True) + EPS))
    r = r_ref[...]
    y_gla = (jnp.concatenate(on, axis=1) * gnorm_ref[...] * (r / (1.0 + jnp.exp(-r)))).astype(BF16)
    up_gla = _dot(y_gla, wup_gla_ref[...])

    qm = qm_ref[...].astype(BF16)
    kv = kv_ref[0]
    om = []
    for h in range(MEM_HEADS):
        hs = slice(h * MEM_HEAD_DIM, (h + 1) * MEM_HEAD_DIM)
        s = _dot_nt(qm[:, hs], kv[:, hs]) * (MEM_HEAD_DIM ** -0.5)
        e = jnp.exp(s - jnp.max(s, axis=-1, keepdims=True))
        p = (e / jnp.sum(e, axis=-1, keepdims=True)).astype(BF16)
        om.append(_dot(p, kv[:, MEM_DIM + h * MEM_HEAD_DIM:MEM_DIM + (h + 1) * MEM_HEAD_DIM]))
    up_mem = _dot(jnp.concatenate(om, axis=1).astype(BF16), wup_mem_ref[...])

    def gate(j):
        return 1.0 / (1.0 + jnp.exp(-gate_ref[:, j * D_MODEL:(j + 1) * D_MODEL]))

    merged = (gate(0) * up_pool + gate(1) * up_gla + gate(2) * up_mem).astype(BF16)
    x1 = x_ref[...] + _dot(merged, wout_ref[...])
    x1_ref[...] = x1

    h2 = _rms(x1, gffn_ref[...])
    h2_ref[...] = h2
    h_hi, h_lo = _split2(h2)
    logits = _dot3(h_hi, h_lo, wr_hi_ref[...], wr_lo_ref[...])
    lane = lax.broadcasted_iota(I32, logits.shape, 1)
    logits = jnp.where(lane < N_EXPERTS, logits, -jnp.inf)
    e = jnp.exp(logits - jnp.max(logits, axis=-1, keepdims=True))
    aff_ref[...] = e / jnp.sum(e, axis=-1, keepdims=True)


def _mix(x2, proj, o_f, o_b, kv, wts, nbatch, seq, tm):
    t = x2.shape[0]
    ns = seq // tm
    h8 = tm // POOL_HALO
    n8 = t // POOL_HALO
    const2 = lambda i: (0, 0)
    in_specs = [
        pl.BlockSpec((tm, D_MODEL), lambda i: (i, 0)),
        pl.BlockSpec((tm, N_BRANCH * D_MODEL), lambda i: (i, COL_GATE // (N_BRANCH * D_MODEL))),
        pl.BlockSpec((tm, POOL_DIM), lambda i: (i, COL_POOL // POOL_DIM)),
        pl.BlockSpec((POOL_HALO, POOL_DIM), lambda i: (jnp.maximum(i * h8 - 1, 0), COL_POOL // POOL_DIM)),
        pl.BlockSpec((POOL_HALO, POOL_DIM), lambda i: (jnp.minimum((i + 1) * h8, n8 - 1), COL_POOL // POOL_DIM)),
        pl.BlockSpec((tm, GLA_V), lambda i: (i, COL_R // GLA_V)),
        pl.BlockSpec((tm, MEM_DIM), lambda i: (i, COL_QM // MEM_DIM)),
        pl.BlockSpec((tm, GLA_V), lambda i: (i, 0)),
        pl.BlockSpec((tm, GLA_V), lambda i: (i, 0)),
        pl.BlockSpec((1, N_MEM, 2 * MEM_DIM), lambda i: (i // ns, 0, 0)),
        pl.BlockSpec((len(POOL_WINDOWS), POOL_GROUP_DIM, POOL_GROUP_DIM), lambda i: (0, 0, 0)),
        pl.BlockSpec((1, POOL_DIM), const2),
        pl.BlockSpec((1, GLA_V), const2),
        pl.BlockSpec((POOL_DIM, D_MODEL), const2),
        pl.BlockSpec((GLA_V, D_MODEL), const2),
        pl.BlockSpec((MEM_DIM, D_MODEL), const2),
        pl.BlockSpec((D_MODEL, D_MODEL), const2),
        pl.BlockSpec((1, D_MODEL), const2),
        pl.BlockSpec((D_MODEL, LANES), const2),
        pl.BlockSpec((D_MODEL, LANES), const2),
    ]
    out_specs = [
        pl.BlockSpec((tm, D_MODEL), lambda i: (i, 0)),
        pl.BlockSpec((tm, D_MODEL), lambda i: (i, 0)),
        pl.BlockSpec((tm, LANES), lambda i: (i, 0)),
    ]
    return pl.pallas_call(
        functools.partial(_mix_kernel, tm=tm, seq=seq),
        out_shape=[jax.ShapeDtypeStruct((t, D_MODEL), F32), jax.ShapeDtypeStruct((t, D_MODEL), F32),
                   jax.ShapeDtypeStruct((t, LANES), F32)],
        grid=(t // tm,),
        in_specs=in_specs,
        out_specs=out_specs,
        scratch_shapes=[pltpu.VMEM((tm + 2 * POOL_HALO, POOL_DIM), F32)],
        compiler_params=_cparams(("arbitrary",)),
        name="mix",
    )(x2, proj, proj, proj, proj, proj, proj, o_f, o_b, kv, *wts)


def _route_kernel(abt_ref, atb_ref, idx_ref, gsel_ref, thr_ref, *, nb, cap, sl):
    ne = N_EXPERTS
    bits_all = lax.bitcast_convert_type(abt_ref[...], I32)

    def bit_step(i, thr):
        cand = thr | (jnp.int32(1) << (30 - i))
        hit = jnp.where(bits_all >= cand, 1.0, 0.0)
        cnt = jnp.sum(jnp.sum(hit, axis=1, keepdims=True), axis=2, keepdims=True)
        return jnp.where(cnt >= cap, cand, thr)

    thr = lax.fori_loop(0, 31, bit_step, jnp.zeros((ne, 1, 1), I32))
    thr_ref[...] = jnp.broadcast_to(thr, thr_ref.shape)

    t_row = lax.broadcasted_iota(I32, (LANES, LANES), 0)
    t_col = lax.broadcasted_iota(I32, (LANES, LANES), 1)
    incl_tt = jnp.where(t_col <= t_row, 1.0, 0.0).astype(BF16)
    ones_t = jnp.ones((8, LANES), BF16)
    ones_tt = jnp.ones((LANES, LANES), BF16)
    b_row = lax.broadcasted_iota(I32, (nb, nb), 0)
    b_col = lax.broadcasted_iota(I32, (nb, nb), 1)
    before_col = jnp.where(b_col < b_row, 1.0, 0.0).astype(BF16)
    before_row = jnp.where(b_row < b_col, 1.0, 0.0).astype(BF16)
    t_sub = lax.broadcasted_iota(I32, (LANES, sl), 0).astype(F32)
    b_sub = lax.broadcasted_iota(I32, (nb, sl), 0).astype(F32)
    lane_sl = lax.broadcasted_iota(I32, (1, sl), 1).astype(F32)

    def expert(e, carry):
        thr_bt = thr_ref[e][0:1, :]
        a_bt = abt_ref[e]
        a_tb = atb_ref[e]
        bits_bt = lax.bitcast_convert_type(a_bt, I32)
        bits_tb = lax.bitcast_convert_type(a_tb, I32)
        thr_tb = thr_ref[e][:, 0:1][0:1, :]
        gt_bt = jnp.where(bits_bt > thr_bt, 1.0, 0.0)
        eq_bt = jnp.where(bits_bt == thr_bt, 1.0, 0.0)
        gt_tb = jnp.where(bits_tb > thr_tb, 1.0, 0.0)
        eq_tb = jnp.where(bits_tb == thr_tb, 1.0, 0.0)
        n_gt = jnp.sum(jnp.sum(gt_bt, axis=0, keepdims=True), axis=1, keepdims=True)
        need = cap - n_gt

        tot_gt_c = _dot(gt_bt.astype(BF16), ones_tt)
        tot_eq_c = _dot(eq_bt.astype(BF16), ones_tt)
        ex_gt_c = _dot(before_col, tot_gt_c.astype(BF16))
        ex_eq_c = _dot(before_col, tot_eq_c.astype(BF16))
        excl_c = ex_gt_c + jnp.minimum(ex_eq_c, need)
        incl_c = ex_gt_c + tot_gt_c + jnp.minimum(ex_eq_c + tot_eq_c, need)
        tot_c = incl_c - excl_c
        tot_gt_r = _dot(ones_t, gt_tb.astype(BF16))
        tot_eq_r = _dot(ones_t, eq_tb.astype(BF16))
        ex_gt_r = _dot(tot_gt_r.astype(BF16), before_row)[0:1, :]
        ex_eq_r = _dot(tot_eq_r.astype(BF16), before_row)[0:1, :]
        excl_r = ex_gt_r + jnp.minimum(ex_eq_r, need)
        rel_gt = _dot(incl_tt, gt_tb.astype(BF16))
        rel_eq = _dot(incl_tt, eq_tb.astype(BF16))
        p_incl = ex_gt_r + rel_gt + jnp.minimum(ex_eq_r + rel_eq, need)
        p_rel = (p_incl - excl_r).astype(BF16)
        a_hi, a_mid, a_lo = _split3(a_tb)
        lhs = jnp.concatenate([p_rel, a_hi, a_mid, a_lo], axis=0)

        for j0 in range(0, cap, sl):
            j = lane_sl + float(j0)
            full = jnp.where(incl_c[:, 0:1] <= j, 1.0, 0.0)
            nblk = jnp.sum(full, axis=0, keepdims=True)
            base = jnp.sum(full * tot_c[:, 0:1], axis=0, keepdims=True)
            sel = jnp.where(b_sub == nblk, 1.0, 0.0).astype(BF16)
            got = _dot(lhs, sel)
            jr = j - base
            within = jnp.sum(jnp.where(got[0:LANES, :] <= jr, 1.0, 0.0), axis=0, keepdims=True)
            aval = got[LANES:2 * LANES, :] + got[2 * LANES:3 * LANES, :] + got[3 * LANES:, :]
            gval = jnp.sum(jnp.where(t_sub == within, aval, 0.0), axis=0, keepdims=True)
            idx_ref[pl.ds(e, 1), j0:j0 + sl] = (nblk * float(LANES) + within).astype(I32)
            gsel_ref[pl.ds(e, 1), j0:j0 + sl] = gval
        return carry

    lax.fori_loop(0, ne, expert, 0)


def _route(a_bt, a_tb, cap):
    ne, nb, _ = a_bt.shape
    sl = min(cap, 512)
    return pl.pallas_call(
        functools.partial(_route_kernel, nb=nb, cap=cap, sl=sl),
        out_shape=[jax.ShapeDtypeStruct((ne, cap), I32), jax.ShapeDtypeStruct((ne, cap), F32)],
        grid=(1,),
        in_specs=[pl.BlockSpec((ne, nb, LANES), lambda i: (0, 0, 0)),
                  pl.BlockSpec((ne, LANES, nb), lambda i: (0, 0, 0))],
        out_specs=[pl.BlockSpec((ne, cap), lambda i: (0, 0)), pl.BlockSpec((ne, cap), lambda i: (0, 0))],
        scratch_shapes=[pltpu.VMEM((ne, 8, LANES), I32)],
        compiler_params=_cparams(("arbitrary",)),
        name="route",
    )(a_bt, a_tb)


def _ffn_kernel(idx_ref, g_ref, wg_ref, wu_ref, wd_ref, h2_hbm, acc_in_hbm, acc_hbm,
                xbuf, abuf, sem_x, sem_a, sem_o, *, ts):
    del acc_in_hbm

    def x_copy(i):
        return pltpu.make_async_copy(h2_hbm.at[pl.ds(idx_ref[0, 0, i], 1), :], xbuf.at[pl.ds(i, 1), :], sem_x)

    def a_copy(i):
        return pltpu.make_async_copy(acc_hbm.at[pl.ds(idx_ref[0, 0, i], 1), :], abuf.at[pl.ds(i, 1), :], sem_a)

    def o_copy(i):
        return pltpu.make_async_copy(abuf.at[pl.ds(i, 1), :], acc_hbm.at[pl.ds(idx_ref[0, 0, i], 1), :], sem_o)

    def start_gather(i, c):
        x_copy(i).start()
        a_copy(i).start()
        return c

    def wait_gather(i, c):
        x_copy(i).wait()
        a_copy(i).wait()
        return c

    lax.fori_loop(0, ts, start_gather, 0)
    lax.fori_loop(0, ts, wait_gather, 0)

    x = xbuf[...].astype(BF16)
    hg = _dot(x, wg_ref[0])
    hu = _dot(x, wu_ref[0])
    hid = (hg / (1.0 + jnp.exp(-hg)) * hu).astype(BF16)
    abuf[...] = abuf[...] + _dot(hid, wd_ref[0]) * g_ref[0]

    def start_scatter(i, c):
        o_copy(i).start()
        return c

    def wait_scatter(i, c):
        o_copy(i).wait()
        return c

    lax.fori_loop(0, ts, start_scatter, 0)
    lax.fori_loop(0, ts, wait_scatter, 0)


def _ffn(idx, gsel, h2, x1, w_gate, w_up, w_down, ts):
    ne, cap = idx.shape
    t = h2.shape[0]
    nt = cap // ts
    idx3 = idx.reshape(ne * nt, 1, ts)
    g3 = gsel.reshape(ne * nt, ts, 1)
    wspec = pl.BlockSpec((1, D_MODEL, D_MODEL), lambda e, j: (e, 0, 0))
    return pl.pallas_call(
        functools.partial(_ffn_kernel, ts=ts),
        out_shape=jax.ShapeDtypeStruct((t, D_MODEL), F32),
        grid=(ne, nt),
        in_specs=[
            pl.BlockSpec((1, 1, ts), lambda e, j: (e * nt + j, 0, 0), memory_space=pltpu.SMEM),
            pl.BlockSpec((1, ts, 1), lambda e, j: (e * nt + j, 0, 0)),
            wspec, wspec, wspec,
            pl.BlockSpec(memory_space=pl.ANY),
            pl.BlockSpec(memory_space=pl.ANY),
        ],
        out_specs=pl.BlockSpec(memory_space=pl.ANY),
        scratch_shapes=[pltpu.VMEM((ts, D_MODEL), F32), pltpu.VMEM((ts, D_MODEL), F32),
                        pltpu.SemaphoreType.DMA, pltpu.SemaphoreType.DMA, pltpu.SemaphoreType.DMA],
        input_output_aliases={6: 0},
        compiler_params=_cparams(("arbitrary", "arbitrary")),
        name="ffn",
    )(idx3, g3, w_gate, w_up, w_down, h2, x1)


def _final_kernel(x_ref, g_ref, o_ref):
    o_ref[...] = _rms(x_ref[...], g_ref[...])


def _final(x2, g, tm):
    t = x2.shape[0]
    return pl.pallas_call(
        _final_kernel,
        out_shape=jax.ShapeDtypeStruct((t, D_MODEL), F32),
        grid=(t // tm,),
        in_specs=[pl.BlockSpec((tm, D_MODEL), lambda i: (i, 0)), pl.BlockSpec((1, D_MODEL), lambda i: (0, 0))],
        out_specs=pl.BlockSpec((tm, D_MODEL), lambda i: (i, 0)),
        compiler_params=_cparams(("arbitrary",)),
        name="final_norm",
    )(x2, g)


def _prep_weights(norm_mix_g, w_in, gla_w2_f, gla_b_f, gla_w2_b, gla_b_b, gla_norm_g, pool_w, pool_scale,
                  mem_norm_g, w_mem_kv, w_up_pool, w_up_gla, w_up_mem, w_out, norm_ffn_g, w_router,
                  w_e_gate, w_e_up, w_e_down, norm_final_g):
    o_pool, o_q, o_k, o_v, o_r = 0, 512, 768, 1024, 1536
    o_lf, o_lb, o_qm, o_gate, o_end = 2048, 2064, 2080, 2592, 5664
    w = w_in[0]
    pad = jnp.zeros((D_MODEL, LR_WIDTH - 2 * GLA_GATE_RANK), F32)
    w_in_r = jnp.concatenate([w[:, o_gate:o_end], w[:, o_pool:o_q], w[:, o_v:o_r], w[:, o_r:o_lf],
                              w[:, o_qm:o_gate], w[:, o_q:o_k], w[:, o_k:o_v], w[:, o_lf:o_lb],
                              w[:, o_lb:o_qm], pad], axis=1).astype(BF16)

    def pad_w2(w2, row0):
        full = jnp.zeros((LR_WIDTH, GLA_QK), F32).at[row0:row0 + GLA_GATE_RANK].set(w2)
        return _split2(full)

    row = lambda v: v.reshape(1, -1)
    wr = jnp.zeros((D_MODEL, LANES), F32).at[:, :N_EXPERTS].set(w_router[0])
    wr_hi, wr_lo = _split2(wr)
    return dict(
        norm_mix_g=row(norm_mix_g[0]), w_in_r=w_in_r,
        w2f=pad_w2(gla_w2_f[0], 0), bf=row(gla_b_f[0]),
        w2b=pad_w2(gla_w2_b[0], GLA_GATE_RANK), bb=row(gla_b_b[0]),
        mem_norm_g=row(mem_norm_g[0]), w_mem_kv=w_mem_kv[0].astype(BF16),
        mix=(pool_w[0].astype(BF16), row(pool_scale[0]), row(gla_norm_g[0]), w_up_pool[0].astype(BF16),
             w_up_gla[0].astype(BF16), w_up_mem[0].astype(BF16), w_out[0].astype(BF16),
             row(norm_ffn_g[0]), wr_hi, wr_lo),
        w_e_gate=w_e_gate[0].astype(BF16), w_e_up=w_e_up[0].astype(BF16), w_e_down=w_e_down[0].astype(BF16),
        norm_final_g=row(norm_final_g),
    )


def _pick(n, pref):
    for c in pref:
        if n % c == 0:
            return c
    raise ValueError(f"no tile for {n}")


def _trunk(x, mem, p):
    nbatch, seq, _ = x.shape
    t = nbatch * seq
    x2 = x.reshape(t, D_MODEL)
    tm = _pick(seq, (256, 128))
    tb = _pick(seq, (512, 256, 128, 64))
    kv = _mem_kv(mem, p["mem_norm_g"], p["w_mem_kv"])
    proj = _inproj(x2, p["norm_mix_g"], p["w_in_r"], tm)
    o_f, o_b = _gla(proj, p["w2f"], p["bf"], p["w2b"], p["bb"], nbatch, seq, tb)
    x1, h2, aff = _mix(x2, proj, o_f, o_b, kv, p["mix"], nbatch, seq, tm)
    cap = max(1, min(t, EC_CAPACITY_FACTOR * t // N_EXPERTS))
    a_bt = aff[:, :N_EXPERTS].T.reshape(N_EXPERTS, t // LANES, LANES)
    a_tb = jnp.swapaxes(a_bt, 1, 2)
    idx, gsel = _route(a_bt, a_tb, cap)
    acc = _ffn(idx, gsel, h2, x1, p["w_e_gate"], p["w_e_up"], p["w_e_down"], _pick(cap, (256, 128, 64)))
    return _final(acc, p["norm_final_g"], tm).reshape(nbatch, seq, D_MODEL)


def kernel(x_prompt, x_sample, mem_prompt, mem_sample, norm_mix_g, w_in, gla_w2_f, gla_b_f, gla_w2_b, gla_b_b, gla_norm_g, pool_w, pool_scale, mem_norm_g, w_mem_kv, w_up_pool, w_up_gla, w_up_mem, w_out, norm_ffn_g, w_router, w_e_gate, w_e_up, w_e_down, norm_final_g):
    p = _prep_weights(norm_mix_g, w_in, gla_w2_f, gla_b_f, gla_w2_b, gla_b_b, gla_norm_g, pool_w, pool_scale,
                      mem_norm_g, w_mem_kv, w_up_pool, w_up_gla, w_up_mem, w_out, norm_ffn_g, w_router,
                      w_e_gate, w_e_up, w_e_down, norm_final_g)
    return (_trunk(x_prompt, mem_prompt, p), _trunk(x_sample, mem_sample, p))
```

```python
import functools

import jax
import jax.numpy as jnp
from jax import lax
from jax.experimental import pallas as pl
from jax.experimental.pallas import tpu as pltpu

F32 = jnp.float32
BF16 = jnp.bfloat16
I32 = jnp.int32

D_MODEL = 1024
N_MEM = 256
POOL_WINDOWS = (2, 4, 8, 16)
POOL_GROUP_DIM = 128
POOL_DIM = 512
POOL_HALO = 8
GLA_HEADS = 4
GLA_DK = 64
GLA_DV = 128
GLA_QK = 256
GLA_V = 512
GLA_GATE_RANK = 16
GLA_GATE_TAU = 16.0
GLA_CHUNK = 64
MEM_HEADS = 4
MEM_HEAD_DIM = 128
MEM_DIM = 512
N_BRANCH = 3
N_EXPERTS = 16
EC_CAPACITY_FACTOR = 2
EPS = 1e-6

LANES = 128
MXU_DIM = 256

COL_GATE = 0
COL_POOL = COL_GATE + N_BRANCH * D_MODEL
COL_V = COL_POOL + POOL_DIM
COL_R = COL_V + GLA_V
COL_QM = COL_R + GLA_V
COL_Q = COL_QM + MEM_DIM
COL_K = COL_Q + GLA_QK
COL_LR = COL_K + GLA_QK
LR_WIDTH = MXU_DIM
PROJ_DIM = COL_LR + LR_WIDTH
PROJ_CHUNKS = ((0, 1536), (1536, 1536), (3072, 1536), (4608, 1280))

DISPATCH_TILE = 256
SEG_ROWS = 64
ROW_ALIGN = 8
FFN_TILE = 256

VMEM_LIMIT = 56 * 1024 * 1024


def _cparams(semantics):
    return pltpu.CompilerParams(dimension_semantics=semantics, vmem_limit_bytes=VMEM_LIMIT)


def _rms(x, g):
    return x * lax.rsqrt(jnp.mean(x * x, axis=-1, keepdims=True) + EPS) * g


def _split2(x):
    hi = x.astype(BF16)
    lo = (x - hi.astype(F32)).astype(BF16)
    return hi, lo


def _split3(x):
    hi = x.astype(BF16)
    r = x - hi.astype(F32)
    mid = r.astype(BF16)
    lo = (r - mid.astype(F32)).astype(BF16)
    return hi, mid, lo


def _dot(a, b):
    return jnp.dot(a, b, preferred_element_type=F32)


def _dot_nt(a, b):
    return lax.dot_general(a, b, (((1,), (1,)), ((), ())), preferred_element_type=F32)


def _dot_tn(a, b):
    return lax.dot_general(a, b, (((0,), (0,)), ((), ())), preferred_element_type=F32)


def _dot3(a_hi, a_lo, b_hi, b_lo):
    return _dot(a_hi, b_hi) + _dot(a_lo, b_hi) + _dot(a_hi, b_lo)


def _memkv_kernel(mem_ref, g_ref, w_ref, o_ref):
    h = _rms(mem_ref[0], g_ref[...]).astype(BF16)
    o_ref[0] = _dot(h, w_ref[...]).astype(BF16)


def _mem_kv(mem, g, w_kv):
    nb = mem.shape[0]
    return pl.pallas_call(
        _memkv_kernel,
        out_shape=jax.ShapeDtypeStruct((nb, N_MEM, 2 * MEM_DIM), BF16),
        grid=(nb,),
        in_specs=[
            pl.BlockSpec((1, N_MEM, D_MODEL), lambda b: (b, 0, 0)),
            pl.BlockSpec((1, D_MODEL), lambda b: (0, 0)),
            pl.BlockSpec((D_MODEL, 2 * MEM_DIM), lambda b: (0, 0)),
        ],
        out_specs=pl.BlockSpec((1, N_MEM, 2 * MEM_DIM), lambda b: (b, 0, 0)),
        compiler_params=_cparams(("arbitrary",)),
        name="mem_kv",
    )(mem, g, w_kv)


def _inproj_kernel(x_ref, g_ref, w_ref, o_ref):
    h = _rms(x_ref[...], g_ref[...]).astype(BF16)
    for c0, cw in PROJ_CHUNKS:
        o_ref[:, c0:c0 + cw] = _dot(h, w_ref[:, c0:c0 + cw])


def _inproj(x2, g, w_in_r, tm):
    t = x2.shape[0]
    return pl.pallas_call(
        _inproj_kernel,
        out_shape=jax.ShapeDtypeStruct((t, PROJ_DIM), F32),
        grid=(t // tm,),
        in_specs=[
            pl.BlockSpec((tm, D_MODEL), lambda i: (i, 0)),
            pl.BlockSpec((1, D_MODEL), lambda i: (0, 0)),
            pl.BlockSpec((D_MODEL, PROJ_DIM), lambda i: (0, 0)),
        ],
        out_specs=pl.BlockSpec((tm, PROJ_DIM), lambda i: (i, 0)),
        compiler_params=_cparams(("arbitrary",)),
        name="inproj",
    )(x2, g, w_in_r)


def _gla_chunk(q, k, v, lr, w2_hi, w2_lo, bias, st_ref, reverse):
    c = GLA_CHUNK
    lr_hi, lr_lo = _split2(lr)
    z = _dot3(lr_hi, lr_lo, w2_hi, w2_lo) + bias
    log_a = (jnp.minimum(z, 0.0) - jnp.log(1.0 + jnp.exp(-jnp.abs(z)))) * (1.0 / GLA_GATE_TAU)
    row = lax.broadcasted_iota(I32, (c, c), 0)
    col = lax.broadcasted_iota(I32, (c, c), 1)
    keep = (col >= row) if reverse else (col <= row)
    tri = jnp.where(keep, 1.0, 0.0).astype(BF16)
    la_hi, la_lo = _split2(log_a)
    bcum = _dot(tri, la_hi) + _dot(tri, la_lo)
    blast = bcum[0:1, :] if reverse else bcum[c - 1:c, :]
    q_s = (q * (jnp.exp(bcum) * (GLA_DK ** -0.5))).astype(BF16)
    k_s = (k * jnp.exp(-bcum)).astype(BF16)
    k_end = (k * jnp.exp(blast - bcum)).astype(BF16)
    decay = jnp.exp(blast)
    v16 = v.astype(BF16)
    outs = []
    for h in range(GLA_HEADS):
        ks = slice(h * GLA_DK, (h + 1) * GLA_DK)
        vs = slice(h * GLA_DV, (h + 1) * GLA_DV)
        st = st_ref[h]
        attn = jnp.where(keep, _dot_nt(q_s[:, ks], k_s[:, ks]), 0.0).astype(BF16)
        o_h = _dot(attn, v16[:, vs]) + _dot_nt(q_s[:, ks], st.astype(BF16))
        st_ref[h] = st * decay[:, ks] + _dot_tn(v16[:, vs], k_end[:, ks])
        outs.append(o_h)
    return jnp.concatenate(outs, axis=1)


def _gla_kernel(qf, kf, vf, lf, qb, kb, vb, lb, w2f_hi, w2f_lo, bf_ref, w2b_hi, w2b_lo, bb_ref,
                of_ref, ob_ref, sf_ref, sb_ref, *, nchunk):
    @pl.when(pl.program_id(1) == 0)
    def _():
        sf_ref[...] = jnp.zeros_like(sf_ref)
        sb_ref[...] = jnp.zeros_like(sb_ref)

    def body(c, carry):
        rf = pl.multiple_of(c * GLA_CHUNK, GLA_CHUNK)
        rb = pl.multiple_of((nchunk - 1 - c) * GLA_CHUNK, GLA_CHUNK)
        sl_f = pl.ds(rf, GLA_CHUNK)
        sl_b = pl.ds(rb, GLA_CHUNK)
        of_ref[sl_f, :] = _gla_chunk(qf[sl_f, :], kf[sl_f, :], vf[sl_f, :], lf[sl_f, :],
                                     w2f_hi[...], w2f_lo[...], bf_ref[...], sf_ref, False)
        ob_ref[sl_b, :] = _gla_chunk(qb[sl_b, :], kb[sl_b, :], vb[sl_b, :], lb[sl_b, :],
                                     w2b_hi[...], w2b_lo[...], bb_ref[...], sb_ref, True)
        return carry

    lax.fori_loop(0, nchunk, body, 0)


def _gla(proj, w2f, bf, w2b, bb, nbatch, seq, tb):
    t = proj.shape[0]
    ns = seq // tb
    w_q, w_v = GLA_QK, GLA_V

    def fwd(cb):
        return lambda b, i: (b * ns + i, cb)

    def bwd(cb):
        return lambda b, i: (b * ns + ns - 1 - i, cb)

    const = lambda b, i: (0, 0)
    in_specs = [
        pl.BlockSpec((tb, w_q), fwd(COL_Q // w_q)),
        pl.BlockSpec((tb, w_q), fwd(COL_K // w_q)),
        pl.BlockSpec((tb, w_v), fwd(COL_V // w_v)),
        pl.BlockSpec((tb, LR_WIDTH), fwd(COL_LR // LR_WIDTH)),
        pl.BlockSpec((tb, w_q), bwd(COL_Q // w_q)),
        pl.BlockSpec((tb, w_q), bwd(COL_K // w_q)),
        pl.BlockSpec((tb, w_v), bwd(COL_V // w_v)),
        pl.BlockSpec((tb, LR_WIDTH), bwd(COL_LR // LR_WIDTH)),
        pl.BlockSpec((LR_WIDTH, GLA_QK), const),
        pl.BlockSpec((LR_WIDTH, GLA_QK), const),
        pl.BlockSpec((1, GLA_QK), const),
        pl.BlockSpec((LR_WIDTH, GLA_QK), const),
        pl.BlockSpec((LR_WIDTH, GLA_QK), const),
        pl.BlockSpec((1, GLA_QK), const),
    ]
    out_specs = [
        pl.BlockSpec((tb, w_v), lambda b, i: (b * ns + i, 0)),
        pl.BlockSpec((tb, w_v), lambda b, i: (b * ns + ns - 1 - i, 0)),
    ]
    w2f_hi, w2f_lo = w2f
    w2b_hi, w2b_lo = w2b
    return pl.pallas_call(
        functools.partial(_gla_kernel, nchunk=tb // GLA_CHUNK),
        out_shape=[jax.ShapeDtypeStruct((t, w_v), F32), jax.ShapeDtypeStruct((t, w_v), F32)],
        grid=(nbatch, ns),
        in_specs=in_specs,
        out_specs=out_specs,
        scratch_shapes=[pltpu.VMEM((GLA_HEADS, GLA_DV, GLA_DK), F32),
                        pltpu.VMEM((GLA_HEADS, GLA_DV, GLA_DK), F32)],
        compiler_params=_cparams(("arbitrary", "arbitrary")),
        name="gla",
    )(proj, proj, proj, proj, proj, proj, proj, proj, w2f_hi, w2f_lo, bf, w2b_hi, w2b_lo, bb)


def _mix_kernel(x_ref, gate_ref, pool_ref, pprev_ref, pnext_ref, r_ref, qm_ref, of_ref, ob_ref,
                kv_ref, poolw_ref, pscale_ref, gnorm_ref, wup_pool_ref, wup_gla_ref, wup_mem_ref,
                wout_ref, gffn_ref, wr_hi_ref, wr_lo_ref,
                x1_ref, h2_ref, aff_ref, pp_ref, *, tm, seq):
    s0 = (pl.program_id(0) % (seq // tm)) * tm
    hal = POOL_HALO

    rows = lax.broadcasted_iota(I32, (hal, 1), 0)
    pp_ref[0:hal, :] = jnp.where(s0 - hal + rows >= 0, pprev_ref[...], 0.0)
    pp_ref[hal:hal + tm, :] = pool_ref[...]
    pp_ref[hal + tm:hal + tm + hal, :] = jnp.where(s0 + tm + rows < seq, pnext_ref[...], 0.0)
    pos = s0 + lax.broadcasted_iota(I32, (tm, 1), 0)
    ypool = []
    for g, w in enumerate(POOL_WINDOWS):
        cs = slice(g * POOL_GROUP_DIM, (g + 1) * POOL_GROUP_DIM)
        tot = pp_ref[hal - w // 2:hal - w // 2 + tm, cs]
        for off in range(-(w // 2) + 1, w - w // 2):
            tot = tot + pp_ref[hal + off:hal + off + tm, cs]
        cnt = (jnp.minimum(pos + (w - w // 2), seq) - jnp.maximum(pos - w // 2, 0)).astype(F32)
        dlt = (tot / cnt - pp_ref[hal:hal + tm, cs]).astype(BF16)
        ypool.append(_dot(dlt, poolw_ref[g]))
    y_pool = (jnp.concatenate(ypool, axis=1) * pscale_ref[...]).astype(BF16)
    up_pool = _dot(y_pool, wup_pool_ref[...])

    o = of_ref[...] + ob_ref[...]
    on = []
    for h in range(GLA_HEADS):
        oh = o[:, h * GLA_DV:(h + 1) * GLA_DV]
        on.append(oh * lax.rsqrt(jnp.mean(oh * oh, axis=-1, keepdims=True) + EPS))
    r = r_ref[...]
    y_gla = (jnp.concatenate(on, axis=1) * gnorm_ref[...] * (r / (1.0 + jnp.exp(-r)))).astype(BF16)
    up_gla = _dot(y_gla, wup_gla_ref[...])

    qm = qm_ref[...].astype(BF16)
    kv = kv_ref[0]
    om = []
    for h in range(MEM_HEADS):
        hs = slice(h * MEM_HEAD_DIM, (h + 1) * MEM_HEAD_DIM)
        s = _dot_nt(qm[:, hs], kv[:, hs]) * (MEM_HEAD_DIM ** -0.5)
        e = jnp.exp(s - jnp.max(s, axis=-1, keepdims=True))
        p = (e / jnp.sum(e, axis=-1, keepdims=True)).astype(BF16)
        om.append(_dot(p, kv[:, MEM_DIM + h * MEM_HEAD_DIM:MEM_DIM + (h + 1) * MEM_HEAD_DIM]))
    up_mem = _dot(jnp.concatenate(om, axis=1).astype(BF16), wup_mem_ref[...])

    def gate(j):
        return 1.0 / (1.0 + jnp.exp(-gate_ref[:, j * D_MODEL:(j + 1) * D_MODEL]))

    merged = (gate(0) * up_pool + gate(1) * up_gla + gate(2) * up_mem).astype(BF16)
    x1 = x_ref[...] + _dot(merged, wout_ref[...])
    x1_ref[...] = x1

    h2 = _rms(x1, gffn_ref[...])
    h_hi, h_lo = _split2(h2)
    h2_ref[...] = h_hi
    logits = _dot3(h_hi, h_lo, wr_hi_ref[...], wr_lo_ref[...])
    lane = lax.broadcasted_iota(I32, logits.shape, 1)
    logits = jnp.where(lane < N_EXPERTS, logits, -jnp.inf)
    e = jnp.exp(logits - jnp.max(logits, axis=-1, keepdims=True))
    aff_ref[...] = e / jnp.sum(e, axis=-1, keepdims=True)


def _mix(x2, proj, o_f, o_b, kv, wts, nbatch, seq, tm):
    t = x2.shape[0]
    ns = seq // tm
    h8 = tm // POOL_HALO
    n8 = t // POOL_HALO
    const2 = lambda i: (0, 0)
    in_specs = [
        pl.BlockSpec((tm, D_MODEL), lambda i: (i, 0)),
        pl.BlockSpec((tm, N_BRANCH * D_MODEL), lambda i: (i, COL_GATE // (N_BRANCH * D_MODEL))),
        pl.BlockSpec((tm, POOL_DIM), lambda i: (i, COL_POOL // POOL_DIM)),
        pl.BlockSpec((POOL_HALO, POOL_DIM), lambda i: (jnp.maximum(i * h8 - 1, 0), COL_POOL // POOL_DIM)),
        pl.BlockSpec((POOL_HALO, POOL_DIM), lambda i: (jnp.minimum((i + 1) * h8, n8 - 1), COL_POOL // POOL_DIM)),
        pl.BlockSpec((tm, GLA_V), lambda i: (i, COL_R // GLA_V)),
        pl.BlockSpec((tm, MEM_DIM), lambda i: (i, COL_QM // MEM_DIM)),
        pl.BlockSpec((tm, GLA_V), lambda i: (i, 0)),
        pl.BlockSpec((tm, GLA_V), lambda i: (i, 0)),
        pl.BlockSpec((1, N_MEM, 2 * MEM_DIM), lambda i: (i // ns, 0, 0)),
        pl.BlockSpec((len(POOL_WINDOWS), POOL_GROUP_DIM, POOL_GROUP_DIM), lambda i: (0, 0, 0)),
        pl.BlockSpec((1, POOL_DIM), const2),
        pl.BlockSpec((1, GLA_V), const2),
        pl.BlockSpec((POOL_DIM, D_MODEL), const2),
        pl.BlockSpec((GLA_V, D_MODEL), const2),
        pl.BlockSpec((MEM_DIM, D_MODEL), const2),
        pl.BlockSpec((D_MODEL, D_MODEL), const2),
        pl.BlockSpec((1, D_MODEL), const2),
        pl.BlockSpec((D_MODEL, LANES), const2),
        pl.BlockSpec((D_MODEL, LANES), const2),
    ]
    out_specs = [
        pl.BlockSpec((tm, D_MODEL), lambda i: (i, 0)),
        pl.BlockSpec((tm, D_MODEL), lambda i: (i, 0)),
        pl.BlockSpec((tm, LANES), lambda i: (i, 0)),
    ]
    return pl.pallas_call(
        functools.partial(_mix_kernel, tm=tm, seq=seq),
        out_shape=[jax.ShapeDtypeStruct((t, D_MODEL), F32), jax.ShapeDtypeStruct((t, D_MODEL), BF16),
                   jax.ShapeDtypeStruct((t, LANES), F32)],
        grid=(t // tm,),
        in_specs=in_specs,
        out_specs=out_specs,
        scratch_shapes=[pltpu.VMEM((tm + 2 * POOL_HALO, POOL_DIM), F32)],
        compiler_params=_cparams(("arbitrary",)),
        name="mix",
    )(x2, proj, proj, proj, proj, proj, proj, o_f, o_b, kv, *wts)


def _route_kernel(abt_ref, sel_ref, slo_ref, cnt_ref, thr_ref, *, nt, td, cap):
    ne = N_EXPERTS
    bits_all = lax.bitcast_convert_type(abt_ref[...], I32)

    def bit_step(i, thr):
        cand = thr | (jnp.int32(1) << (30 - i))
        hit = jnp.where(bits_all >= cand, 1.0, 0.0)
        cnt = jnp.sum(jnp.sum(hit, axis=1, keepdims=True), axis=2, keepdims=True)
        return jnp.where(cnt >= cap, cand, thr)

    thr = lax.fori_loop(0, 31, bit_step, jnp.zeros((ne, 1, 1), I32))
    thr_ref[...] = jnp.broadcast_to(thr, thr_ref.shape)

    t_row = lax.broadcasted_iota(I32, (td, td), 0)
    t_col = lax.broadcasted_iota(I32, (td, td), 1)
    upto = jnp.where(t_row <= t_col, 1.0, 0.0).astype(BF16)
    ones_t = jnp.ones((td, LANES), BF16)
    i_row = lax.broadcasted_iota(I32, (nt, nt), 0)
    i_col = lax.broadcasted_iota(I32, (nt, nt), 1)
    before = jnp.where(i_col < i_row, 1.0, 0.0).astype(BF16)

    def expert(e, carry):
        thr_e = thr_ref[e][0:1, :]
        bits = lax.bitcast_convert_type(abt_ref[e], I32)
        gt = jnp.where(bits > thr_e, 1.0, 0.0)
        eq = jnp.where(bits == thr_e, 1.0, 0.0)
        n_gt = jnp.sum(jnp.sum(gt, axis=0, keepdims=True), axis=1, keepdims=True)
        need = cap - n_gt
        eq16 = eq.astype(BF16)
        tot_eq = _dot(eq16, ones_t)
        ex_eq = _dot(before, tot_eq.astype(BF16))
        rel_eq = _dot(eq16, upto)
        sel = gt + eq * jnp.where(ex_eq[:, 0:1] + rel_eq <= need, 1.0, 0.0)
        sel_ref[e] = sel
        cnt = _dot(sel.astype(BF16), ones_t)
        pad = jnp.floor((cnt + float(ROW_ALIGN - 1)) * (1.0 / ROW_ALIGN)) * float(ROW_ALIGN)
        slo_ref[e] = _dot(before, pad.astype(BF16)).astype(I32)
        cnt_ref[e] = cnt.astype(I32)
        return carry

    lax.fori_loop(0, ne, expert, 0)


def _route(a_bt, cap):
    ne, nt, td = a_bt.shape
    return pl.pallas_call(
        functools.partial(_route_kernel, nt=nt, td=td, cap=cap),
        out_shape=[jax.ShapeDtypeStruct((ne, nt, td), F32), jax.ShapeDtypeStruct((ne, nt, LANES), I32),
                   jax.ShapeDtypeStruct((ne, nt, LANES), I32)],
        grid=(1,),
        in_specs=[pl.BlockSpec((ne, nt, td), lambda i: (0, 0, 0))],
        out_specs=[pl.BlockSpec((ne, nt, td), lambda i: (0, 0, 0)), pl.BlockSpec((ne, nt, LANES), lambda i: (0, 0, 0)),
                   pl.BlockSpec((ne, nt, LANES), lambda i: (0, 0, 0))],
        scratch_shapes=[pltpu.VMEM((ne, 8, td), I32)],
        compiler_params=_cparams(("arbitrary",)),
        name="route",
    )(a_bt)


def _segment_onehot(sel, chunk):
    td = sel.shape[1]
    t_row = lax.broadcasted_iota(I32, (td, td), 0)
    t_col = lax.broadcasted_iota(I32, (td, td), 1)
    earlier = jnp.where(t_row < t_col, 1.0, 0.0).astype(BF16)
    rank = _dot(sel.astype(BF16), earlier)
    r_iota = lax.broadcasted_iota(I32, (SEG_ROWS, td), 0).astype(F32) + float(chunk * SEG_ROWS)
    blocks = [jnp.where(rank[e:e + 1, :] == r_iota, sel[e:e + 1, :], 0.0) for e in range(N_EXPERTS)]
    return jnp.concatenate(blocks, axis=0).astype(BF16)


def _dispatch_kernel(slo_ref, cnt_ref, vend_ref, sel_ref, h2_ref, aff_ref, xe_hbm, ge_hbm,
                     xbuf, gbuf, xov, gov, sem_x, sem_g, sem_ov, *, nt, cpad):
    i = pl.program_id(0)
    slot = i % 2

    @pl.when(i == 0)
    def _():
        xov[...] = jnp.zeros_like(xov)
        gov[...] = jnp.zeros_like(gov)

        def fill(e, first, rows, wait):
            row0 = pl.multiple_of(first, ROW_ALIGN)
            cx = pltpu.make_async_copy(xov.at[pl.ds(0, rows), :], xe_hbm.at[pl.ds(row0, rows), :], sem_ov)
            cg = pltpu.make_async_copy(gov.at[pl.ds(0, rows), :], ge_hbm.at[pl.ds(row0, rows), :], sem_ov)
            if wait:
                cx.wait()
                cg.wait()
            else:
                cx.start()
                cg.start()

        for wait in (False, True):
            for e in range(N_EXPERTS):
                tail0 = e * cpad + vend_ref[e]
                n_big = (cpad - vend_ref[e]) // SEG_ROWS
                n_small = ((cpad - vend_ref[e]) % SEG_ROWS) // ROW_ALIGN

                def big(k, c, tail0=tail0, e=e, wait=wait):
                    fill(e, tail0 + k * SEG_ROWS, SEG_ROWS, wait)
                    return c

                def small(k, c, tail0=tail0, n_big=n_big, e=e, wait=wait):
                    fill(e, tail0 + n_big * SEG_ROWS + k * ROW_ALIGN, ROW_ALIGN, wait)
                    return c

                lax.fori_loop(0, n_big, big, 0)
                lax.fori_loop(0, n_small, small, 0)

    def seg_copies(step, buf_slot, e):
        row0 = pl.multiple_of(slo_ref[step, e], ROW_ALIGN) + e * cpad
        src = pl.ds(e * SEG_ROWS, SEG_ROWS)
        return (pltpu.make_async_copy(xbuf.at[buf_slot, src, :], xe_hbm.at[pl.ds(row0, SEG_ROWS), :], sem_x),
                pltpu.make_async_copy(gbuf.at[buf_slot, src, :], ge_hbm.at[pl.ds(row0, SEG_ROWS), :], sem_g))

    def wait_step(step, buf_slot):
        for e in range(N_EXPERTS):
            cx, cg = seg_copies(step, buf_slot, e)
            cx.wait()
            cg.wait()

    @pl.when(i > 0)
    def _():
        wait_step(i - 1, 1 - slot)

    sel = sel_ref[...]
    h2 = h2_ref[...]
    a_hi, a_mid, a_lo = _split3(aff_ref[...])
    lane = lax.broadcasted_iota(I32, (SEG_ROWS, LANES), 1)

    def rows_for(chunk):
        oh = _segment_onehot(sel, chunk)
        xr = _dot(oh, h2)
        g3 = _dot(oh, a_hi) + _dot(oh, a_mid) + _dot(oh, a_lo)
        gs = []
        for e in range(N_EXPERTS):
            ge = g3[e * SEG_ROWS:(e + 1) * SEG_ROWS, :]
            gs.append(jnp.broadcast_to(jnp.sum(jnp.where(lane == e, ge, 0.0), axis=1, keepdims=True),
                                       (SEG_ROWS, LANES)))
        return xr, jnp.concatenate(gs, axis=0)

    xr, gr = rows_for(0)
    xbuf[slot] = xr
    gbuf[slot] = gr
    for e in range(N_EXPERTS):
        cx, cg = seg_copies(i, slot, e)
        cx.start()
        cg.start()

    cmax = cnt_ref[i, 0]
    for e in range(1, N_EXPERTS):
        cmax = jnp.maximum(cmax, cnt_ref[i, e])
    for chunk in range(1, DISPATCH_TILE // SEG_ROWS):
        @pl.when(cmax > chunk * SEG_ROWS)
        def _(chunk=chunk):
            xo, go = rows_for(chunk)
            xov[...] = xo
            gov[...] = go
            for e in range(N_EXPERTS):
                @pl.when(cnt_ref[i, e] > chunk * SEG_ROWS)
                def _(e=e):
                    row0 = pl.multiple_of(slo_ref[i, e], ROW_ALIGN) + e * cpad + chunk * SEG_ROWS
                    src = pl.ds(e * SEG_ROWS, SEG_ROWS)
                    cx = pltpu.make_async_copy(xov.at[src, :], xe_hbm.at[pl.ds(row0, SEG_ROWS), :], sem_ov)
                    cg = pltpu.make_async_copy(gov.at[src, :], ge_hbm.at[pl.ds(row0, SEG_ROWS), :], sem_ov)
                    cx.start()
                    cg.start()
                    cx.wait()
                    cg.wait()

    @pl.when(i == nt - 1)
    def _():
        wait_step(i, slot)


def _dispatch(slo_t, cnt_t, vend, sel2, h2, aff, cpad):
    ne, t = sel2.shape
    td = DISPATCH_TILE
    nt = t // td
    grid_spec = pltpu.PrefetchScalarGridSpec(
        num_scalar_prefetch=3,
        grid=(nt,),
        in_specs=[
            pl.BlockSpec((ne, td), lambda i, s, c, v: (0, i)),
            pl.BlockSpec((td, D_MODEL), lambda i, s, c, v: (i, 0)),
            pl.BlockSpec((td, LANES), lambda i, s, c, v: (i, 0)),
        ],
        out_specs=[pl.BlockSpec(memory_space=pl.ANY), pl.BlockSpec(memory_space=pl.ANY)],
        scratch_shapes=[pltpu.VMEM((2, ne * SEG_ROWS, D_MODEL), F32), pltpu.VMEM((2, ne * SEG_ROWS, LANES), F32),
                        pltpu.VMEM((ne * SEG_ROWS, D_MODEL), F32), pltpu.VMEM((ne * SEG_ROWS, LANES), F32),
                        pltpu.SemaphoreType.DMA, pltpu.SemaphoreType.DMA, pltpu.SemaphoreType.DMA],
    )
    return pl.pallas_call(
        functools.partial(_dispatch_kernel, nt=nt, cpad=cpad),
        out_shape=[jax.ShapeDtypeStruct((ne * cpad, D_MODEL), F32), jax.ShapeDtypeStruct((ne * cpad, LANES), F32)],
        grid_spec=grid_spec,
        compiler_params=_cparams(("arbitrary",)),
        name="dispatch",
    )(slo_t, cnt_t, vend, sel2, h2, aff)


def _ffn_kernel(vend_ref, x_ref, g_ref, wg_ref, wu_ref, wd_ref, o_ref, *, ft):
    e = pl.program_id(0)
    row0 = pl.program_id(1) * ft
    vend = vend_ref[e]

    @pl.when(row0 < vend)
    def _():
        valid = row0 + lax.broadcasted_iota(I32, (ft, 1), 0) < vend
        x = jnp.where(valid, x_ref[...], 0.0).astype(BF16)
        hg = _dot(x, wg_ref[0])
        hu = _dot(x, wu_ref[0])
        hid = (hg / (1.0 + jnp.exp(-hg)) * hu).astype(BF16)
        gate = jnp.where(valid, g_ref[:, 0:1], 0.0)
        o_ref[...] = _dot(hid, wd_ref[0]) * gate

    @pl.when(row0 >= vend)
    def _():
        o_ref[...] = jnp.zeros_like(o_ref)


def _ffn(vend, xe, ge, w_gate, w_up, w_down, cpad, ft):
    ne = N_EXPERTS
    nj = cpad // ft
    wspec = pl.BlockSpec((1, D_MODEL, D_MODEL), lambda e, j, v: (e, 0, 0))
    grid_spec = pltpu.PrefetchScalarGridSpec(
        num_scalar_prefetch=1,
        grid=(ne, nj),
        in_specs=[
            pl.BlockSpec((ft, D_MODEL), lambda e, j, v: (e * nj + j, 0)),
            pl.BlockSpec((ft, LANES), lambda e, j, v: (e * nj + j, 0)),
            wspec, wspec, wspec,
        ],
        out_specs=pl.BlockSpec((ft, D_MODEL), lambda e, j, v: (e * nj + j, 0)),
    )
    return pl.pallas_call(
        functools.partial(_ffn_kernel, ft=ft),
        out_shape=jax.ShapeDtypeStruct((ne * cpad, D_MODEL), F32),
        grid_spec=grid_spec,
        compiler_params=_cparams(("arbitrary", "arbitrary")),
        name="ffn",
    )(vend, xe, ge, w_gate, w_up, w_down)


def _combine_kernel(slo_ref, cnt_ref, sel_ref, x1_ref, gfin_ref, ye_hbm, y_ref, ybuf, yov, acc_ref,
                    sem, sem_ov, *, nt, cpad):
    i = pl.program_id(0)
    slot = i % 2

    def seg_copy(step, buf_slot, e):
        row0 = pl.multiple_of(slo_ref[step, e], ROW_ALIGN) + e * cpad
        return pltpu.make_async_copy(ye_hbm.at[pl.ds(row0, SEG_ROWS), :],
                                     ybuf.at[buf_slot, pl.ds(e * SEG_ROWS, SEG_ROWS), :], sem.at[buf_slot])

    def fetch(step, buf_slot):
        for e in range(N_EXPERTS):
            seg_copy(step, buf_slot, e).start()

    @pl.when(i == 0)
    def _():
        fetch(0, 0)

    @pl.when(i + 1 < nt)
    def _():
        fetch(i + 1, 1 - slot)

    for e in range(N_EXPERTS):
        seg_copy(i, slot, e).wait()

    sel = sel_ref[...]
    acc_ref[...] = x1_ref[...] + _dot_tn(_segment_onehot(sel, 0), ybuf[slot].astype(BF16))

    cmax = cnt_ref[i, 0]
    for e in range(1, N_EXPERTS):
        cmax = jnp.maximum(cmax, cnt_ref[i, e])
    for chunk in range(1, DISPATCH_TILE // SEG_ROWS):
        @pl.when(cmax > chunk * SEG_ROWS)
        def _(chunk=chunk):
            for e in range(N_EXPERTS):
                dst = pl.ds(e * SEG_ROWS, SEG_ROWS)

                @pl.when(cnt_ref[i, e] > chunk * SEG_ROWS)
                def _(e=e, dst=dst):
                    row0 = pl.multiple_of(slo_ref[i, e], ROW_ALIGN) + e * cpad + chunk * SEG_ROWS
                    cp = pltpu.make_async_copy(ye_hbm.at[pl.ds(row0, SEG_ROWS), :], yov.at[dst, :], sem_ov)
                    cp.start()
                    cp.wait()

                @pl.when(cnt_ref[i, e] <= chunk * SEG_ROWS)
                def _(dst=dst):
                    yov[dst, :] = jnp.zeros((SEG_ROWS, D_MODEL), F32)
            acc_ref[...] += _dot_tn(_segment_onehot(sel, chunk), yov[...].astype(BF16))

    y_ref[...] = _rms(acc_ref[...], gfin_ref[...])


def _combine(slo_t, cnt_t, sel2, x1, g_final, ye, cpad):
    ne, t = sel2.shape
    td = DISPATCH_TILE
    nt = t // td
    grid_spec = pltpu.PrefetchScalarGridSpec(
        num_scalar_prefetch=2,
        grid=(nt,),
        in_specs=[
            pl.BlockSpec((ne, td), lambda i, s, c: (0, i)),
            pl.BlockSpec((td, D_MODEL), lambda i, s, c: (i, 0)),
            pl.BlockSpec((1, D_MODEL), lambda i, s, c: (0, 0)),
            pl.BlockSpec(memory_space=pl.ANY),
        ],
        out_specs=pl.BlockSpec((td, D_MODEL), lambda i, s, c: (i, 0)),
        scratch_shapes=[pltpu.VMEM((2, ne * SEG_ROWS, D_MODEL), F32), pltpu.VMEM((ne * SEG_ROWS, D_MODEL), F32),
                        pltpu.VMEM((td, D_MODEL), F32),
                        pltpu.SemaphoreType.DMA((2,)), pltpu.SemaphoreType.DMA],
    )
    return pl.pallas_call(
        functools.partial(_combine_kernel, nt=nt, cpad=cpad),
        out_shape=jax.ShapeDtypeStruct((t, D_MODEL), F32),
        grid_spec=grid_spec,
        compiler_params=_cparams(("arbitrary",)),
        name="combine",
    )(slo_t, cnt_t, sel2, x1, g_final, ye)


def _prep_weights(norm_mix_g, w_in, gla_w2_f, gla_b_f, gla_w2_b, gla_b_b, gla_norm_g, pool_w, pool_scale,
                  mem_norm_g, w_mem_kv, w_up_pool, w_up_gla, w_up_mem, w_out, norm_ffn_g, w_router,
                  w_e_gate, w_e_up, w_e_down, norm_final_g):
    o_pool, o_q, o_k, o_v, o_r = 0, 512, 768, 1024, 1536
    o_lf, o_lb, o_qm, o_gate, o_end = 2048, 2064, 2080, 2592, 5664
    w = w_in[0]
    pad = jnp.zeros((D_MODEL, LR_WIDTH - 2 * GLA_GATE_RANK), F32)
    w_in_r = jnp.concatenate([w[:, o_gate:o_end], w[:, o_pool:o_q], w[:, o_v:o_r], w[:, o_r:o_lf],
                              w[:, o_qm:o_gate], w[:, o_q:o_k], w[:, o_k:o_v], w[:, o_lf:o_lb],
                              w[:, o_lb:o_qm], pad], axis=1).astype(BF16)

    def pad_w2(w2, row0):
        full = jnp.zeros((LR_WIDTH, GLA_QK), F32).at[row0:row0 + GLA_GATE_RANK].set(w2)
        return _split2(full)

    row = lambda v: v.reshape(1, -1)
    wr = jnp.zeros((D_MODEL, LANES), F32).at[:, :N_EXPERTS].set(w_router[0])
    wr_hi, wr_lo = _split2(wr)
    return dict(
        norm_mix_g=row(norm_mix_g[0]), w_in_r=w_in_r,
        w2f=pad_w2(gla_w2_f[0], 0), bf=row(gla_b_f[0]),
        w2b=pad_w2(gla_w2_b[0], GLA_GATE_RANK), bb=row(gla_b_b[0]),
        mem_norm_g=row(mem_norm_g[0]), w_mem_kv=w_mem_kv[0].astype(BF16),
        mix=(pool_w[0].astype(BF16), row(pool_scale[0]), row(gla_norm_g[0]), w_up_pool[0].astype(BF16),
             w_up_gla[0].astype(BF16), w_up_mem[0].astype(BF16), w_out[0].astype(BF16),
             row(norm_ffn_g[0]), wr_hi, wr_lo),
        w_e_gate=w_e_gate[0].astype(BF16), w_e_up=w_e_up[0].astype(BF16), w_e_down=w_e_down[0].astype(BF16),
        norm_final_g=row(norm_final_g),
    )


def _pick(n, pref):
    for c in pref:
        if n % c == 0:
            return c
    raise ValueError(f"no tile for {n}")


def _trunk(x, mem, p):
    nbatch, seq, _ = x.shape
    t = nbatch * seq
    x2 = x.reshape(t, D_MODEL)
    tm = _pick(seq, (256, 128))
    tb = _pick(seq, (512, 256, 128, 64))
    kv = _mem_kv(mem, p["mem_norm_g"], p["w_mem_kv"])
    proj = _inproj(x2, p["norm_mix_g"], p["w_in_r"], tm)
    o_f, o_b = _gla(proj, p["w2f"], p["bf"], p["w2b"], p["bb"], nbatch, seq, tb)
    x1, h2, aff = _mix(x2, proj, o_f, o_b, kv, p["mix"], nbatch, seq, tm)
    cap = max(1, min(t, EC_CAPACITY_FACTOR * t // N_EXPERTS))
    nt = t // DISPATCH_TILE
    sel, slo, cnt = _route(aff[:, :N_EXPERTS].T.reshape(N_EXPERTS, nt, DISPATCH_TILE), cap)
    sel2 = sel.reshape(N_EXPERTS, t)
    slo_t = slo[:, :, 0].T
    cnt_t = cnt[:, :, 0].T
    last_chunks = jnp.maximum(-(-cnt_t[nt - 1] // SEG_ROWS), 1)
    vend = slo_t[nt - 1] + last_chunks * SEG_ROWS
    cpad = -(-(cap + (ROW_ALIGN - 1) * nt + SEG_ROWS) // FFN_TILE) * FFN_TILE
    xe, ge = _dispatch(slo_t, cnt_t, vend, sel2, h2, aff, cpad)
    ye = _ffn(vend, xe, ge, p["w_e_gate"], p["w_e_up"], p["w_e_down"], cpad, FFN_TILE)
    y = _combine(slo_t, cnt_t, sel2, x1, p["norm_final_g"], ye, cpad)
    return y.reshape(nbatch, seq, D_MODEL)


def kernel(x_prompt, x_sample, mem_prompt, mem_sample, norm_mix_g, w_in, gla_w2_f, gla_b_f, gla_w2_b, gla_b_b, gla_norm_g, pool_w, pool_scale, mem_norm_g, w_mem_kv, w_up_pool, w_up_gla, w_up_mem, w_out, norm_ffn_g, w_router, w_e_gate, w_e_up, w_e_down, norm_final_g):
    p = _prep_weights(norm_mix_g, w_in, gla_w2_f, gla_b_f, gla_w2_b, gla_b_b, gla_norm_g, pool_w, pool_scale,
                      mem_norm_g, w_mem_kv, w_up_pool, w_up_gla, w_up_mem, w_out, norm_ffn_g, w_router,
                      w_e_gate, w_e_up, w_e_down, norm_final_g)
    return (_trunk(x_prompt, mem_prompt, p), _trunk(x_sample, mem_sample, p))
```

```python
import functools

import jax
import jax.numpy as jnp
from jax import lax
from jax.experimental import pallas as pl
from jax.experimental.pallas import tpu as pltpu

F32 = jnp.float32
BF16 = jnp.bfloat16
I32 = jnp.int32
U32 = jnp.uint32

D_MODEL = 1024
N_MEM = 256
POOL_WINDOWS = (2, 4, 8, 16)
POOL_GROUP_DIM = 128
POOL_DIM = 512
POOL_HALO = 8
GLA_HEADS = 4
GLA_DK = 64
GLA_DV = 128
GLA_QK = 256
GLA_V = 512
GLA_GATE_RANK = 16
GLA_GATE_TAU = 16.0
GLA_CHUNK = 64
MEM_HEADS = 4
MEM_HEAD_DIM = 128
MEM_DIM = 512
N_BRANCH = 3
N_EXPERTS = 16
EC_CAPACITY_FACTOR = 2
EPS = 1e-6

LANES = 128
MXU_DIM = 256

COL_GATE = 0
COL_POOL = COL_GATE + N_BRANCH * D_MODEL
COL_V = COL_POOL + POOL_DIM
COL_R = COL_V + GLA_V
COL_QM = COL_R + GLA_V
COL_Q = COL_QM + MEM_DIM
COL_K = COL_Q + GLA_QK
COL_LR = COL_K + GLA_QK
LR_WIDTH = MXU_DIM
PROJ_DIM = COL_LR + LR_WIDTH
PROJ_CHUNKS = ((0, 1536), (1536, 1536), (3072, 1536), (4608, 1280))

MIX_SUB = 256
DISPATCH_TILE = 256
SEG_ROWS = 64
ROW_ALIGN = 8
FFN_TILE = 256
PACKED_DIM = D_MODEL // 2

VMEM_LIMIT = 56 * 1024 * 1024


def _cparams(semantics):
    return pltpu.CompilerParams(dimension_semantics=semantics, vmem_limit_bytes=VMEM_LIMIT)


def _rms(x, g):
    return x * lax.rsqrt(jnp.mean(x * x, axis=-1, keepdims=True) + EPS) * g


def _split2(x):
    hi = x.astype(BF16)
    lo = (x - hi.astype(F32)).astype(BF16)
    return hi, lo


def _split3(x):
    hi = x.astype(BF16)
    r = x - hi.astype(F32)
    mid = r.astype(BF16)
    lo = (r - mid.astype(F32)).astype(BF16)
    return hi, mid, lo


def _dot(a, b):
    return jnp.dot(a, b, preferred_element_type=F32)


def _dot_nt(a, b):
    return lax.dot_general(a, b, (((1,), (1,)), ((), ())), preferred_element_type=F32)


def _dot_tn(a, b):
    return lax.dot_general(a, b, (((0,), (0,)), ((), ())), preferred_element_type=F32)


def _dot3(a_hi, a_lo, b_hi, b_lo):
    return _dot(a_hi, b_hi) + _dot(a_lo, b_hi) + _dot(a_hi, b_lo)


def _memkv_kernel(mem_ref, g_ref, w_ref, o_ref):
    h = _rms(mem_ref[0], g_ref[...]).astype(BF16)
    o_ref[0] = _dot(h, w_ref[...]).astype(BF16)


def _mem_kv(mem, g, w_kv):
    nb = mem.shape[0]
    return pl.pallas_call(
        _memkv_kernel,
        out_shape=jax.ShapeDtypeStruct((nb, N_MEM, 2 * MEM_DIM), BF16),
        grid=(nb,),
        in_specs=[
            pl.BlockSpec((1, N_MEM, D_MODEL), lambda b: (b, 0, 0)),
            pl.BlockSpec((1, D_MODEL), lambda b: (0, 0)),
            pl.BlockSpec((D_MODEL, 2 * MEM_DIM), lambda b: (0, 0)),
        ],
        out_specs=pl.BlockSpec((1, N_MEM, 2 * MEM_DIM), lambda b: (b, 0, 0)),
        compiler_params=_cparams(("arbitrary",)),
        name="mem_kv",
    )(mem, g, w_kv)


def _inproj_kernel(x_ref, g_ref, w_ref, o_ref):
    h = _rms(x_ref[...], g_ref[...]).astype(BF16)
    for c0, cw in PROJ_CHUNKS:
        o_ref[:, c0:c0 + cw] = _dot(h, w_ref[:, c0:c0 + cw])


def _inproj(x2, g, w_in_r, tm):
    t = x2.shape[0]
    return pl.pallas_call(
        _inproj_kernel,
        out_shape=jax.ShapeDtypeStruct((t, PROJ_DIM), F32),
        grid=(t // tm,),
        in_specs=[
            pl.BlockSpec((tm, D_MODEL), lambda i: (i, 0)),
            pl.BlockSpec((1, D_MODEL), lambda i: (0, 0)),
            pl.BlockSpec((D_MODEL, PROJ_DIM), lambda i: (0, 0)),
        ],
        out_specs=pl.BlockSpec((tm, PROJ_DIM), lambda i: (i, 0)),
        compiler_params=_cparams(("arbitrary",)),
        name="inproj",
    )(x2, g, w_in_r)


def _gla_kernel(qf, kf, vf, lf, qb, kb, vb, lb, w2f_hi, w2f_lo, bf_ref, w2b_hi, w2b_lo, bb_ref,
                of_ref, ob_ref, sf_ref, sb_ref, *, nchunk):
    @pl.when(pl.program_id(1) == 0)
    def _():
        sf_ref[...] = jnp.zeros_like(sf_ref)
        sb_ref[...] = jnp.zeros_like(sb_ref)

    c = GLA_CHUNK
    pr = 2 * c
    dirs = (0, 1)
    heads = range(GLA_HEADS)
    ks = [slice(h * GLA_DK, (h + 1) * GLA_DK) for h in heads]
    vs = [slice(h * GLA_DV, (h + 1) * GLA_DV) for h in heads]
    pairs = [slice(p * pr, (p + 1) * pr) for p in range(nchunk // 2)]
    chunks = [slice(j * c, (j + 1) * c) for j in range(nchunk)]
    q = (qf[...], qb[...])
    k = (kf[...], kb[...])
    v16 = (vf[...].astype(BF16), vb[...].astype(BF16))
    st_refs = (sf_ref, sb_ref)

    lr = [_split2(l[...]) for l in (lf, lb)]
    w2 = ((w2f_hi[...], w2f_lo[...]), (w2b_hi[...], w2b_lo[...]))
    bias = (bf_ref[...], bb_ref[...])
    z = [_dot3(lr[d][0], lr[d][1], w2[d][0], w2[d][1]) + bias[d] for d in dirs]
    log_a = [(jnp.minimum(z[d], 0.0) - jnp.log(1.0 + jnp.exp(-jnp.abs(z[d])))) * (1.0 / GLA_GATE_TAU)
             for d in dirs]

    row = lax.broadcasted_iota(I32, (c, c), 0)
    col = lax.broadcasted_iota(I32, (c, c), 1)
    tri = (jnp.where(col <= row, 1.0, 0.0).astype(BF16), jnp.where(col >= row, 1.0, 0.0).astype(BF16))
    la = [_split2(log_a[d]) for d in dirs]
    bcum_c = [[_dot(tri[d], la[d][0][r]) + _dot(tri[d], la[d][1][r]) for r in chunks] for d in dirs]
    blast_c = [[bc[c - 1:c, :] if d == 0 else bc[0:1, :] for bc in bcum_c[d]] for d in dirs]
    bcum = [jnp.concatenate(bcum_c[d], axis=0) for d in dirs]
    blast = [jnp.concatenate([jnp.broadcast_to(b, (c, GLA_QK)) for b in blast_c[d]], axis=0) for d in dirs]
    q_s = [(q[d] * (jnp.exp(bcum[d]) * (GLA_DK ** -0.5))).astype(BF16) for d in dirs]
    k_s = [(k[d] * jnp.exp(-bcum[d])).astype(BF16) for d in dirs]
    k_end = [(k[d] * jnp.exp(blast[d] - bcum[d])).astype(BF16) for d in dirs]

    prow = lax.broadcasted_iota(I32, (pr, pr), 0)
    pcol = lax.broadcasted_iota(I32, (pr, pr), 1)
    in_chunk = prow & (c - 1)
    keep = ((prow - pcol).astype(jnp.uint32) <= in_chunk.astype(jnp.uint32),
            (pcol - prow).astype(jnp.uint32) <= (c - 1 - in_chunk).astype(jnp.uint32))
    attn = [[[jnp.where(keep[d], _dot_nt(q_s[d][r, ks[h]], k_s[d][r, ks[h]]), 0.0).astype(BF16)
              for r in pairs] for h in heads] for d in dirs]
    kv = [[[_dot_tn(v16[d][r, vs[h]], k_end[d][r, ks[h]]) for r in chunks] for h in heads]
          for d in dirs]
    intra = [[jnp.concatenate([_dot(attn[d][h][p], v16[d][r, vs[h]]) for p, r in enumerate(pairs)], axis=0)
              for h in heads] for d in dirs]

    st = [[st_refs[d][h] for h in heads] for d in dirs]
    inter = [[[None] * nchunk for _ in heads] for _ in dirs]
    for step in range(nchunk):
        for d in dirs:
            j = step if d == 0 else nchunk - 1 - step
            dec = jnp.exp(blast_c[d][j])
            for h in heads:
                inter[d][h][j] = _dot_nt(q_s[d][chunks[j], ks[h]], st[d][h].astype(BF16))
                st[d][h] = st[d][h] * dec[:, ks[h]] + kv[d][h][j]
    for d, o_ref in zip(dirs, (of_ref, ob_ref)):
        for h in heads:
            st_refs[d][h] = st[d][h]
        o_ref[...] = jnp.concatenate([intra[d][h] + jnp.concatenate(inter[d][h], axis=0) for h in heads],
                                     axis=1)


def _gla(proj, w2f, bf, w2b, bb, nbatch, seq, tb):
    t = proj.shape[0]
    ns = seq // tb
    w_q, w_v = GLA_QK, GLA_V

    def fwd(cb):
        return lambda b, i: (b * ns + i, cb)

    def bwd(cb):
        return lambda b, i: (b * ns + ns - 1 - i, cb)

    const = lambda b, i: (0, 0)
    in_specs = [
        pl.BlockSpec((tb, w_q), fwd(COL_Q // w_q)),
        pl.BlockSpec((tb, w_q), fwd(COL_K // w_q)),
        pl.BlockSpec((tb, w_v), fwd(COL_V // w_v)),
        pl.BlockSpec((tb, LR_WIDTH), fwd(COL_LR // LR_WIDTH)),
        pl.BlockSpec((tb, w_q), bwd(COL_Q // w_q)),
        pl.BlockSpec((tb, w_q), bwd(COL_K // w_q)),
        pl.BlockSpec((tb, w_v), bwd(COL_V // w_v)),
        pl.BlockSpec((tb, LR_WIDTH), bwd(COL_LR // LR_WIDTH)),
        pl.BlockSpec((LR_WIDTH, GLA_QK), const),
        pl.BlockSpec((LR_WIDTH, GLA_QK), const),
        pl.BlockSpec((1, GLA_QK), const),
        pl.BlockSpec((LR_WIDTH, GLA_QK), const),
        pl.BlockSpec((LR_WIDTH, GLA_QK), const),
        pl.BlockSpec((1, GLA_QK), const),
    ]
    out_specs = [
        pl.BlockSpec((tb, w_v), lambda b, i: (b * ns + i, 0)),
        pl.BlockSpec((tb, w_v), lambda b, i: (b * ns + ns - 1 - i, 0)),
    ]
    w2f_hi, w2f_lo = w2f
    w2b_hi, w2b_lo = w2b
    return pl.pallas_call(
        functools.partial(_gla_kernel, nchunk=tb // GLA_CHUNK),
        out_shape=[jax.ShapeDtypeStruct((t, w_v), F32), jax.ShapeDtypeStruct((t, w_v), F32)],
        grid=(nbatch, ns),
        in_specs=in_specs,
        out_specs=out_specs,
        scratch_shapes=[pltpu.VMEM((GLA_HEADS, GLA_DV, GLA_DK), F32),
                        pltpu.VMEM((GLA_HEADS, GLA_DV, GLA_DK), F32)],
        compiler_params=_cparams(("arbitrary", "arbitrary")),
        name="gla",
    )(proj, proj, proj, proj, proj, proj, proj, proj, w2f_hi, w2f_lo, bf, w2b_hi, w2b_lo, bb)


def _mix_kernel(x_ref, gate_ref, pool_ref, pprev_ref, pnext_ref, r_ref, qm_ref, of_ref, ob_ref,
                kv_ref, poolw_ref, pscale_ref, gnorm_ref, wup_pool_ref, wup_gla_ref, wup_mem_ref,
                wout_ref, gffn_ref, wr_hi_ref, wr_lo_ref,
                x1_ref, h2_ref, aff_ref, pp_ref, *, tm, seq):
    s0 = (pl.program_id(0) % (seq // tm)) * tm
    hal = POOL_HALO

    rows = lax.broadcasted_iota(I32, (hal, 1), 0)
    pp_ref[0:hal, :] = jnp.where(s0 - hal + rows >= 0, pprev_ref[...], 0.0)
    pp_ref[hal:hal + tm, :] = pool_ref[...]
    pp_ref[hal + tm:hal + tm + hal, :] = jnp.where(s0 + tm + rows < seq, pnext_ref[...], 0.0)
    subs = [slice(r0, r0 + MIX_SUB) for r0 in range(0, tm, MIX_SUB)]
    kv = kv_ref[0]

    def branches(rs):
        pos = s0 + rs.start + lax.broadcasted_iota(I32, (MIX_SUB, 1), 0)
        base = hal + rs.start
        ypool = []
        for g, w in enumerate(POOL_WINDOWS):
            cs = slice(g * POOL_GROUP_DIM, (g + 1) * POOL_GROUP_DIM)
            tot = pp_ref[base - w // 2:base - w // 2 + MIX_SUB, cs]
            for off in range(-(w // 2) + 1, w - w // 2):
                tot = tot + pp_ref[base + off:base + off + MIX_SUB, cs]
            cnt = (jnp.minimum(pos + (w - w // 2), seq) - jnp.maximum(pos - w // 2, 0)).astype(F32)
            dlt = (tot / cnt - pp_ref[base:base + MIX_SUB, cs]).astype(BF16)
            ypool.append(_dot(dlt, poolw_ref[g]))
        y_pool = (jnp.concatenate(ypool, axis=1) * pscale_ref[...]).astype(BF16)
        up_pool = _dot(y_pool, wup_pool_ref[...])

        o = of_ref[rs, :] + ob_ref[rs, :]
        on = []
        for h in range(GLA_HEADS):
            oh = o[:, h * GLA_DV:(h + 1) * GLA_DV]
            on.append(oh * lax.rsqrt(jnp.mean(oh * oh, axis=-1, keepdims=True) + EPS))
        r = r_ref[rs, :]
        silu = (r * 0.5) * (1.0 + jnp.tanh(r * 0.5))
        y_gla = (jnp.concatenate(on, axis=1) * gnorm_ref[...] * silu).astype(BF16)
        up_gla = _dot(y_gla, wup_gla_ref[...])

        qm = qm_ref[rs, :].astype(BF16)
        om = []
        for h in range(MEM_HEADS):
            hs = slice(h * MEM_HEAD_DIM, (h + 1) * MEM_HEAD_DIM)
            s = _dot_nt(qm[:, hs], kv[:, hs]) * (MEM_HEAD_DIM ** -0.5)
            e = jnp.exp(s - jnp.max(s, axis=-1, keepdims=True))
            p = (e / jnp.sum(e, axis=-1, keepdims=True)).astype(BF16)
            om.append(_dot(p, kv[:, MEM_DIM + h * MEM_HEAD_DIM:MEM_DIM + (h + 1) * MEM_HEAD_DIM]))
        up_mem = _dot(jnp.concatenate(om, axis=1).astype(BF16), wup_mem_ref[...])
        return up_pool, up_gla, up_mem

    def merge(rs, ups):
        gated = [y + jnp.tanh(gate_ref[rs, j * D_MODEL:(j + 1) * D_MODEL] * 0.5) * y for j, y in enumerate(ups)]
        merged = ((gated[0] + gated[1] + gated[2]) * 0.5).astype(BF16)
        x1 = x_ref[rs, :] + _dot(merged, wout_ref[...])
        x1_ref[rs, :] = x1
        return x1

    def route(rs, x1):
        h2 = _rms(x1, gffn_ref[...])
        h_hi, h_lo = _split2(h2)
        h2_ref[rs, :] = h_hi
        wr_hi, wr_lo = wr_hi_ref[...], wr_lo_ref[...]
        logits = _dot_nt(wr_hi, h_hi) + _dot_nt(wr_hi, h_lo) + _dot_nt(wr_lo, h_hi)
        e = jnp.exp(logits - jnp.max(logits, axis=0, keepdims=True))
        aff_ref[:, rs] = e / jnp.sum(e, axis=0, keepdims=True)

    ups = [branches(rs) for rs in subs]
    x1s = [merge(rs, u) for rs, u in zip(subs, ups)]
    for rs, x1 in zip(subs, x1s):
        route(rs, x1)


def _mix(x2, proj, o_f, o_b, kv, wts, nbatch, seq, tm):
    t = x2.shape[0]
    ns = seq // tm
    h8 = tm // POOL_HALO
    n8 = t // POOL_HALO
    const2 = lambda i: (0, 0)
    in_specs = [
        pl.BlockSpec((tm, D_MODEL), lambda i: (i, 0)),
        pl.BlockSpec((tm, N_BRANCH * D_MODEL), lambda i: (i, COL_GATE // (N_BRANCH * D_MODEL))),
        pl.BlockSpec((tm, POOL_DIM), lambda i: (i, COL_POOL // POOL_DIM)),
        pl.BlockSpec((POOL_HALO, POOL_DIM), lambda i: (jnp.maximum(i * h8 - 1, 0), COL_POOL // POOL_DIM)),
        pl.BlockSpec((POOL_HALO, POOL_DIM), lambda i: (jnp.minimum((i + 1) * h8, n8 - 1), COL_POOL // POOL_DIM)),
        pl.BlockSpec((tm, GLA_V), lambda i: (i, COL_R // GLA_V)),
        pl.BlockSpec((tm, MEM_DIM), lambda i: (i, COL_QM // MEM_DIM)),
        pl.BlockSpec((tm, GLA_V), lambda i: (i, 0)),
        pl.BlockSpec((tm, GLA_V), lambda i: (i, 0)),
        pl.BlockSpec((1, N_MEM, 2 * MEM_DIM), lambda i: (i // ns, 0, 0)),
        pl.BlockSpec((len(POOL_WINDOWS), POOL_GROUP_DIM, POOL_GROUP_DIM), lambda i: (0, 0, 0)),
        pl.BlockSpec((1, POOL_DIM), const2),
        pl.BlockSpec((1, GLA_V), const2),
        pl.BlockSpec((POOL_DIM, D_MODEL), const2),
        pl.BlockSpec((GLA_V, D_MODEL), const2),
        pl.BlockSpec((MEM_DIM, D_MODEL), const2),
        pl.BlockSpec((D_MODEL, D_MODEL), const2),
        pl.BlockSpec((1, D_MODEL), const2),
        pl.BlockSpec((N_EXPERTS, D_MODEL), const2),
        pl.BlockSpec((N_EXPERTS, D_MODEL), const2),
    ]
    out_specs = [
        pl.BlockSpec((tm, D_MODEL), lambda i: (i, 0)),
        pl.BlockSpec((tm, D_MODEL), lambda i: (i, 0)),
        pl.BlockSpec((N_EXPERTS, tm), lambda i: (0, i)),
    ]
    return pl.pallas_call(
        functools.partial(_mix_kernel, tm=tm, seq=seq),
        out_shape=[jax.ShapeDtypeStruct((t, D_MODEL), F32), jax.ShapeDtypeStruct((t, D_MODEL), BF16),
                   jax.ShapeDtypeStruct((N_EXPERTS, t), F32)],
        grid=(t // tm,),
        in_specs=in_specs,
        out_specs=out_specs,
        scratch_shapes=[pltpu.VMEM((tm + 2 * POOL_HALO, POOL_DIM), F32)],
        compiler_params=_cparams(("arbitrary",)),
        name="mix",
    )(x2, proj, proj, proj, proj, proj, proj, o_f, o_b, kv, *wts)


def _route_kernel(abt_ref, sel_ref, slo_ref, cnt_ref, thr_ref, *, nt, td, cap):
    ne = N_EXPERTS
    bits_all = lax.bitcast_convert_type(abt_ref[...], I32)

    def bit_step(i, thr):
        cand = thr | (jnp.int32(1) << (30 - i))
        hit = jnp.where(bits_all >= cand, 1.0, 0.0)
        cnt = jnp.sum(jnp.sum(hit, axis=1, keepdims=True), axis=2, keepdims=True)
        return jnp.where(cnt >= cap, cand, thr)

    thr = lax.fori_loop(0, 31, bit_step, jnp.zeros((ne, 1, 1), I32))
    thr_ref[...] = jnp.broadcast_to(thr, thr_ref.shape)

    t_row = lax.broadcasted_iota(I32, (td, td), 0)
    t_col = lax.broadcasted_iota(I32, (td, td), 1)
    upto = jnp.where(t_row <= t_col, 1.0, 0.0).astype(BF16)
    ones_t = jnp.ones((td, LANES), BF16)
    i_row = lax.broadcasted_iota(I32, (nt, nt), 0)
    i_col = lax.broadcasted_iota(I32, (nt, nt), 1)
    before = jnp.where(i_col < i_row, 1.0, 0.0).astype(BF16)

    def expert(e, carry):
        thr_e = thr_ref[e][0:1, :]
        bits = lax.bitcast_convert_type(abt_ref[e], I32)
        gt = jnp.where(bits > thr_e, 1.0, 0.0)
        eq = jnp.where(bits == thr_e, 1.0, 0.0)
        n_gt = jnp.sum(jnp.sum(gt, axis=0, keepdims=True), axis=1, keepdims=True)
        need = cap - n_gt
        eq16 = eq.astype(BF16)
        tot_eq = _dot(eq16, ones_t)
        ex_eq = _dot(before, tot_eq.astype(BF16))
        rel_eq = _dot(eq16, upto)
        sel = gt + eq * jnp.where(ex_eq[:, 0:1] + rel_eq <= need, 1.0, 0.0)
        sel_ref[e] = sel
        cnt = _dot(sel.astype(BF16), ones_t)
        pad = jnp.floor((cnt + float(ROW_ALIGN - 1)) * (1.0 / ROW_ALIGN)) * float(ROW_ALIGN)
        slo_ref[e] = _dot(before, pad.astype(BF16)).astype(I32)
        cnt_ref[e] = cnt.astype(I32)
        return carry

    lax.fori_loop(0, ne, expert, 0)


def _route(a_bt, cap):
    ne, nt, td = a_bt.shape
    return pl.pallas_call(
        functools.partial(_route_kernel, nt=nt, td=td, cap=cap),
        out_shape=[jax.ShapeDtypeStruct((ne, nt, td), F32), jax.ShapeDtypeStruct((ne, nt, LANES), I32),
                   jax.ShapeDtypeStruct((ne, nt, LANES), I32)],
        grid=(1,),
        in_specs=[pl.BlockSpec((ne, nt, td), lambda i: (0, 0, 0))],
        out_specs=[pl.BlockSpec((ne, nt, td), lambda i: (0, 0, 0)), pl.BlockSpec((ne, nt, LANES), lambda i: (0, 0, 0)),
                   pl.BlockSpec((ne, nt, LANES), lambda i: (0, 0, 0))],
        scratch_shapes=[pltpu.VMEM((ne, 8, td), I32)],
        compiler_params=_cparams(("arbitrary",)),
        name="route",
    )(a_bt)


def _segment_blocks(sel, chunk):
    td = sel.shape[1]
    t_row = lax.broadcasted_iota(I32, (td, td), 0)
    t_col = lax.broadcasted_iota(I32, (td, td), 1)
    earlier = jnp.where(t_row < t_col, 1.0, 0.0).astype(BF16)
    rank = _dot(sel.astype(BF16), earlier)
    r_iota = lax.broadcasted_iota(I32, (SEG_ROWS, td), 0).astype(F32) + float(chunk * SEG_ROWS)
    return [jnp.where(rank[e:e + 1, :] == r_iota, sel[e:e + 1, :], 0.0) for e in range(N_EXPERTS)]


def _segment_onehot(sel, chunk):
    return jnp.concatenate(_segment_blocks(sel, chunk), axis=0).astype(BF16)


def _pack_bf16_pairs(x, is_bf16_valued):
    m = x.shape[1] // 2
    if not is_bf16_valued:
        x = x.astype(BF16).astype(F32)
    bits = lax.bitcast_convert_type(x, jnp.uint32)
    return (bits[:, :m] >> 16) | (bits[:, m:] & jnp.uint32(0xFFFF0000))


def _unpack_bf16_pairs(w):
    lo = lax.bitcast_convert_type(w << 16, F32)
    hi = lax.bitcast_convert_type(w & jnp.uint32(0xFFFF0000), F32)
    return jnp.concatenate([lo, hi], axis=1).astype(BF16)


def _dispatch_kernel(slo_ref, cnt_ref, vend_ref, sel_ref, h2_ref, aff_ref, xe_hbm, ge_hbm,
                     xbuf, gbuf, xov, gov, sem_x, sem_g, sem_ov, *, nt, cpad):
    i = pl.program_id(0)
    slot = i % 2

    @pl.when(i == 0)
    def _():
        xov[...] = jnp.zeros_like(xov)
        gov[...] = jnp.zeros_like(gov)

        def fill(e, first, rows, wait):
            row0 = pl.multiple_of(first, ROW_ALIGN)
            cx = pltpu.make_async_copy(xov.at[pl.ds(0, rows), :], xe_hbm.at[pl.ds(row0, rows), :], sem_ov)
            cg = pltpu.make_async_copy(gov.at[pl.ds(0, rows), :], ge_hbm.at[pl.ds(row0, rows), :], sem_ov)
            if wait:
                cx.wait()
                cg.wait()
            else:
                cx.start()
                cg.start()

        for wait in (False, True):
            for e in range(N_EXPERTS):
                tail0 = e * cpad + vend_ref[e]
                n_big = (cpad - vend_ref[e]) // SEG_ROWS
                n_small = ((cpad - vend_ref[e]) % SEG_ROWS) // ROW_ALIGN

                def big(k, c, tail0=tail0, e=e, wait=wait):
                    fill(e, tail0 + k * SEG_ROWS, SEG_ROWS, wait)
                    return c

                def small(k, c, tail0=tail0, n_big=n_big, e=e, wait=wait):
                    fill(e, tail0 + n_big * SEG_ROWS + k * ROW_ALIGN, ROW_ALIGN, wait)
                    return c

                lax.fori_loop(0, n_big, big, 0)
                lax.fori_loop(0, n_small, small, 0)

    def seg_copies(step, buf_slot, e):
        row0 = pl.multiple_of(slo_ref[step, e], ROW_ALIGN) + e * cpad
        src = pl.ds(e * SEG_ROWS, SEG_ROWS)
        return (pltpu.make_async_copy(xbuf.at[buf_slot, src, :], xe_hbm.at[pl.ds(row0, SEG_ROWS), :], sem_x),
                pltpu.make_async_copy(gbuf.at[buf_slot, src, :], ge_hbm.at[pl.ds(row0, SEG_ROWS), :], sem_g))

    def wait_step(step, buf_slot):
        for e in range(N_EXPERTS):
            cx, cg = seg_copies(step, buf_slot, e)
            cx.wait()
            cg.wait()

    @pl.when(i > 0)
    def _():
        wait_step(i - 1, 1 - slot)

    sel = sel_ref[...]
    h2 = h2_ref[...]
    aff = aff_ref[...]

    def rows_for(chunk):
        blocks = _segment_blocks(sel, chunk)
        xr = _pack_bf16_pairs(_dot(jnp.concatenate(blocks, axis=0).astype(BF16), h2), True)
        gs = [jnp.broadcast_to(jnp.sum(blocks[e] * aff[e:e + 1, :], axis=1, keepdims=True), (SEG_ROWS, LANES))
              for e in range(N_EXPERTS)]
        return xr, jnp.concatenate(gs, axis=0)

    xr, gr = rows_for(0)
    xbuf[slot] = xr
    gbuf[slot] = gr
    for e in range(N_EXPERTS):
        cx, cg = seg_copies(i, slot, e)
        cx.start()
        cg.start()

    cmax = cnt_ref[i, 0]
    for e in range(1, N_EXPERTS):
        cmax = jnp.maximum(cmax, cnt_ref[i, e])
    for chunk in range(1, DISPATCH_TILE // SEG_ROWS):
        @pl.when(cmax > chunk * SEG_ROWS)
        def _(chunk=chunk):
            xo, go = rows_for(chunk)
            xov[...] = xo
            gov[...] = go
            for e in range(N_EXPERTS):
                @pl.when(cnt_ref[i, e] > chunk * SEG_ROWS)
                def _(e=e):
                    row0 = pl.multiple_of(slo_ref[i, e], ROW_ALIGN) + e * cpad + chunk * SEG_ROWS
                    src = pl.ds(e * SEG_ROWS, SEG_ROWS)
                    cx = pltpu.make_async_copy(xov.at[src, :], xe_hbm.at[pl.ds(row0, SEG_ROWS), :], sem_ov)
                    cg = pltpu.make_async_copy(gov.at[src, :], ge_hbm.at[pl.ds(row0, SEG_ROWS), :], sem_ov)
                    cx.start()
                    cg.start()
                    cx.wait()
                    cg.wait()

    @pl.when(i == nt - 1)
    def _():
        wait_step(i, slot)


def _dispatch(slo_t, cnt_t, vend, sel2, h2, aff, cpad):
    ne, t = sel2.shape
    td = DISPATCH_TILE
    nt = t // td
    grid_spec = pltpu.PrefetchScalarGridSpec(
        num_scalar_prefetch=3,
        grid=(nt,),
        in_specs=[
            pl.BlockSpec((ne, td), lambda i, s, c, v: (0, i)),
            pl.BlockSpec((td, D_MODEL), lambda i, s, c, v: (i, 0)),
            pl.BlockSpec((ne, td), lambda i, s, c, v: (0, i)),
        ],
        out_specs=[pl.BlockSpec(memory_space=pl.ANY), pl.BlockSpec(memory_space=pl.ANY)],
        scratch_shapes=[pltpu.VMEM((2, ne * SEG_ROWS, PACKED_DIM), U32), pltpu.VMEM((2, ne * SEG_ROWS, LANES), F32),
                        pltpu.VMEM((ne * SEG_ROWS, PACKED_DIM), U32), pltpu.VMEM((ne * SEG_ROWS, LANES), F32),
                        pltpu.SemaphoreType.DMA, pltpu.SemaphoreType.DMA, pltpu.SemaphoreType.DMA],
    )
    return pl.pallas_call(
        functools.partial(_dispatch_kernel, nt=nt, cpad=cpad),
        out_shape=[jax.ShapeDtypeStruct((ne * cpad, PACKED_DIM), U32), jax.ShapeDtypeStruct((ne * cpad, LANES), F32)],
        grid_spec=grid_spec,
        compiler_params=_cparams(("arbitrary",)),
        name="dispatch",
    )(slo_t, cnt_t, vend, sel2, h2, aff)


def _ffn_kernel(vend_ref, x_ref, g_ref, wg_ref, wu_ref, wd_ref, o_ref, *, ft):
    e = pl.program_id(0)
    row0 = pl.program_id(1) * ft
    vend = vend_ref[e]

    @pl.when(row0 < vend)
    def _():
        valid = row0 + lax.broadcasted_iota(I32, (ft, 1), 0) < vend
        x = _unpack_bf16_pairs(jnp.where(valid, x_ref[...], jnp.uint32(0)))
        hg = _dot(x, wg_ref[0])
        hu = _dot(x, wu_ref[0])
        hid = ((hg * 0.5) * (1.0 + jnp.tanh(hg * 0.5)) * hu).astype(BF16)
        gate = jnp.where(valid, g_ref[:, 0:1], 0.0)
        o_ref[...] = _pack_bf16_pairs(_dot(hid, wd_ref[0]) * gate, False)

    @pl.when(row0 >= vend)
    def _():
        o_ref[...] = jnp.zeros_like(o_ref)


def _ffn(vend, xe, ge, w_gate, w_up, w_down, cpad, ft):
    ne = N_EXPERTS
    nj = cpad // ft
    wspec = pl.BlockSpec((1, D_MODEL, D_MODEL), lambda e, j, v: (e, 0, 0))
    grid_spec = pltpu.PrefetchScalarGridSpec(
        num_scalar_prefetch=1,
        grid=(ne, nj),
        in_specs=[
            pl.BlockSpec((ft, PACKED_DIM), lambda e, j, v: (e * nj + j, 0)),
            pl.BlockSpec((ft, LANES), lambda e, j, v: (e * nj + j, 0)),
            wspec, wspec, wspec,
        ],
        out_specs=pl.BlockSpec((ft, PACKED_DIM), lambda e, j, v: (e * nj + j, 0)),
    )
    return pl.pallas_call(
        functools.partial(_ffn_kernel, ft=ft),
        out_shape=jax.ShapeDtypeStruct((ne * cpad, PACKED_DIM), U32),
        grid_spec=grid_spec,
        compiler_params=_cparams(("arbitrary", "arbitrary")),
        name="ffn",
    )(vend, xe, ge, w_gate, w_up, w_down)


def _combine_kernel(slo_ref, cnt_ref, sel_ref, x1_ref, gfin_ref, ye_hbm, y_ref, ybuf, yov, acc_ref,
                    sem, sem_ov, *, nt, cpad):
    i = pl.program_id(0)
    slot = i % 2

    def seg_copy(step, buf_slot, e):
        row0 = pl.multiple_of(slo_ref[step, e], ROW_ALIGN) + e * cpad
        return pltpu.make_async_copy(ye_hbm.at[pl.ds(row0, SEG_ROWS), :],
                                     ybuf.at[buf_slot, pl.ds(e * SEG_ROWS, SEG_ROWS), :], sem.at[buf_slot])

    def fetch(step, buf_slot):
        for e in range(N_EXPERTS):
            seg_copy(step, buf_slot, e).start()

    @pl.when(i == 0)
    def _():
        fetch(0, 0)

    @pl.when(i + 1 < nt)
    def _():
        fetch(i + 1, 1 - slot)

    for e in range(N_EXPERTS):
        seg_copy(i, slot, e).wait()

    sel = sel_ref[...]
    acc_ref[...] = x1_ref[...] + _dot_tn(_segment_onehot(sel, 0), _unpack_bf16_pairs(ybuf[slot]))

    cmax = cnt_ref[i, 0]
    for e in range(1, N_EXPERTS):
        cmax = jnp.maximum(cmax, cnt_ref[i, e])
    for chunk in range(1, DISPATCH_TILE // SEG_ROWS):
        @pl.when(cmax > chunk * SEG_ROWS)
        def _(chunk=chunk):
            for e in range(N_EXPERTS):
                dst = pl.ds(e * SEG_ROWS, SEG_ROWS)

                @pl.when(cnt_ref[i, e] > chunk * SEG_ROWS)
                def _(e=e, dst=dst):
                    row0 = pl.multiple_of(slo_ref[i, e], ROW_ALIGN) + e * cpad + chunk * SEG_ROWS
                    cp = pltpu.make_async_copy(ye_hbm.at[pl.ds(row0, SEG_ROWS), :], yov.at[dst, :], sem_ov)
                    cp.start()
                    cp.wait()

                @pl.when(cnt_ref[i, e] <= chunk * SEG_ROWS)
                def _(dst=dst):
                    yov[dst, :] = jnp.zeros((SEG_ROWS, PACKED_DIM), U32)
            acc_ref[...] += _dot_tn(_segment_onehot(sel, chunk), _unpack_bf16_pairs(yov[...]))

    y_ref[...] = _rms(acc_ref[...], gfin_ref[...])


def _combine(slo_t, cnt_t, sel2, x1, g_final, ye, cpad):
    ne, t = sel2.shape
    td = DISPATCH_TILE
    nt = t // td
    grid_spec = pltpu.PrefetchScalarGridSpec(
        num_scalar_prefetch=2,
        grid=(nt,),
        in_specs=[
            pl.BlockSpec((ne, td), lambda i, s, c: (0, i)),
            pl.BlockSpec((td, D_MODEL), lambda i, s, c: (i, 0)),
            pl.BlockSpec((1, D_MODEL), lambda i, s, c: (0, 0)),
            pl.BlockSpec(memory_space=pl.ANY),
        ],
        out_specs=pl.BlockSpec((td, D_MODEL), lambda i, s, c: (i, 0)),
        scratch_shapes=[pltpu.VMEM((2, ne * SEG_ROWS, PACKED_DIM), U32), pltpu.VMEM((ne * SEG_ROWS, PACKED_DIM), U32),
                        pltpu.VMEM((td, D_MODEL), F32),
                        pltpu.SemaphoreType.DMA((2,)), pltpu.SemaphoreType.DMA],
    )
    return pl.pallas_call(
        functools.partial(_combine_kernel, nt=nt, cpad=cpad),
        out_shape=jax.ShapeDtypeStruct((t, D_MODEL), F32),
        grid_spec=grid_spec,
        compiler_params=_cparams(("arbitrary",)),
        name="combine",
    )(slo_t, cnt_t, sel2, x1, g_final, ye)


def _prep_weights(norm_mix_g, w_in, gla_w2_f, gla_b_f, gla_w2_b, gla_b_b, gla_norm_g, pool_w, pool_scale,
                  mem_norm_g, w_mem_kv, w_up_pool, w_up_gla, w_up_mem, w_out, norm_ffn_g, w_router,
                  w_e_gate, w_e_up, w_e_down, norm_final_g):
    o_pool, o_q, o_k, o_v, o_r = 0, 512, 768, 1024, 1536
    o_lf, o_lb, o_qm, o_gate, o_end = 2048, 2064, 2080, 2592, 5664
    w = w_in[0]
    pad = jnp.zeros((D_MODEL, LR_WIDTH - 2 * GLA_GATE_RANK), F32)
    w_in_r = jnp.concatenate([w[:, o_gate:o_end], w[:, o_pool:o_q], w[:, o_v:o_r], w[:, o_r:o_lf],
                              w[:, o_qm:o_gate], w[:, o_q:o_k], w[:, o_k:o_v], w[:, o_lf:o_lb],
                              w[:, o_lb:o_qm], pad], axis=1).astype(BF16)

    def pad_w2(w2, row0):
        full = jnp.zeros((LR_WIDTH, GLA_QK), F32).at[row0:row0 + GLA_GATE_RANK].set(w2)
        return _split2(full)

    row = lambda v: v.reshape(1, -1)
    wr_hi, wr_lo = _split2(w_router[0].T)
    return dict(
        norm_mix_g=row(norm_mix_g[0]), w_in_r=w_in_r,
        w2f=pad_w2(gla_w2_f[0], 0), bf=row(gla_b_f[0]),
        w2b=pad_w2(gla_w2_b[0], GLA_GATE_RANK), bb=row(gla_b_b[0]),
        mem_norm_g=row(mem_norm_g[0]), w_mem_kv=w_mem_kv[0].astype(BF16),
        mix=(pool_w[0].astype(BF16), row(pool_scale[0]), row(gla_norm_g[0]), w_up_pool[0].astype(BF16),
             w_up_gla[0].astype(BF16), w_up_mem[0].astype(BF16), w_out[0].astype(BF16),
             row(norm_ffn_g[0]), wr_hi, wr_lo),
        w_e_gate=w_e_gate[0].astype(BF16), w_e_up=w_e_up[0].astype(BF16), w_e_down=w_e_down[0].astype(BF16),
        norm_final_g=row(norm_final_g),
    )


def _pick(n, pref):
    for c in pref:
        if n % c == 0:
            return c
    raise ValueError(f"no tile for {n}")


def _trunk(x, mem, p):
    nbatch, seq, _ = x.shape
    t = nbatch * seq
    x2 = x.reshape(t, D_MODEL)
    tm = _pick(seq, (256, 128))
    tb = _pick(seq, (512, 256, 128))
    kv = _mem_kv(mem, p["mem_norm_g"], p["w_mem_kv"])
    proj = _inproj(x2, p["norm_mix_g"], p["w_in_r"], tm)
    o_f, o_b = _gla(proj, p["w2f"], p["bf"], p["w2b"], p["bb"], nbatch, seq, tb)
    x1, h2, aff = _mix(x2, proj, o_f, o_b, kv, p["mix"], nbatch, seq, _pick(seq, (2 * MIX_SUB, MIX_SUB)))
    cap = max(1, min(t, EC_CAPACITY_FACTOR * t // N_EXPERTS))
    nt = t // DISPATCH_TILE
    sel, slo, cnt = _route(aff.reshape(N_EXPERTS, nt, DISPATCH_TILE), cap)
    sel2 = sel.reshape(N_EXPERTS, t)
    slo_t = slo[:, :, 0].T
    cnt_t = cnt[:, :, 0].T
    last_chunks = jnp.maximum(-(-cnt_t[nt - 1] // SEG_ROWS), 1)
    vend = slo_t[nt - 1] + last_chunks * SEG_ROWS
    cpad = -(-(cap + (ROW_ALIGN - 1) * nt + SEG_ROWS) // FFN_TILE) * FFN_TILE
    xe, ge = _dispatch(slo_t, cnt_t, vend, sel2, h2, aff, cpad)
    ye = _ffn(vend, xe, ge, p["w_e_gate"], p["w_e_up"], p["w_e_down"], cpad, FFN_TILE)
    y = _combine(slo_t, cnt_t, sel2, x1, p["norm_final_g"], ye, cpad)
    return y.reshape(nbatch, seq, D_MODEL)


def kernel(x_prompt, x_sample, mem_prompt, mem_sample, norm_mix_g, w_in, gla_w2_f, gla_b_f, gla_w2_b, gla_b_b, gla_norm_g, pool_w, pool_scale, mem_norm_g, w_mem_kv, w_up_pool, w_up_gla, w_up_mem, w_out, norm_ffn_g, w_router, w_e_gate, w_e_up, w_e_down, norm_final_g):
    p = _prep_weights(norm_mix_g, w_in, gla_w2_f, gla_b_f, gla_w2_b, gla_b_b, gla_norm_g, pool_w, pool_scale,
                      mem_norm_g, w_mem_kv, w_up_pool, w_up_gla, w_up_mem, w_out, norm_ffn_g, w_router,
                      w_e_gate, w_e_up, w_e_down, norm_final_g)
    return (_trunk(x_prompt, mem_prompt, p), _trunk(x_sample, mem_sample, p))
```

```python
import functools

import jax
import jax.numpy as jnp
from jax import lax
from jax.experimental import pallas as pl
from jax.experimental.pallas import tpu as pltpu

F32 = jnp.float32
BF16 = jnp.bfloat16
I32 = jnp.int32

D_MODEL = 1024
N_MEM = 256
POOL_WINDOWS = (2, 4, 8, 16)
POOL_GROUP_DIM = 128
POOL_DIM = 512
POOL_PAD = 64
GLA_HEADS = 4
GLA_DK = 64
GLA_DV = 128
GLA_QK = 256
GLA_V = 512
GLA_GATE_RANK = 16
GLA_GATE_TAU = 16.0
GLA_CHUNK = 64
MEM_HEADS = 4
MEM_HEAD_DIM = 128
MEM_DIM = 512
N_BRANCH = 3
N_EXPERTS = 16
EC_CAPACITY_FACTOR = 2
EPS = 1e-6

LANES = 128
MXU_DIM = 256

COL_GATE = 0
COL_POOL = COL_GATE + N_BRANCH * D_MODEL
COL_V = COL_POOL + POOL_DIM
COL_R = COL_V + GLA_V
COL_QM = COL_R + GLA_V
COL_Q = COL_QM + MEM_DIM
COL_K = COL_Q + GLA_QK
COL_LR = COL_K + GLA_QK
LR_WIDTH = MXU_DIM
PROJ_DIM = COL_LR + LR_WIDTH
PROJ_CHUNKS = ((0, 1536), (1536, 1536), (3072, 1536), (4608, 1280))

MIX_SUB = 256
DISPATCH_TILE = 256
SEG_ROWS = 48
ROW_ALIGN = 16
FFN_TILE = 512

VMEM_LIMIT = 56 * 1024 * 1024


def _cparams(semantics):
    return pltpu.CompilerParams(dimension_semantics=semantics, vmem_limit_bytes=VMEM_LIMIT)


def _rms(x, g):
    return x * lax.rsqrt(jnp.mean(x * x, axis=-1, keepdims=True) + EPS) * g


def _split2(x):
    hi = x.astype(BF16)
    lo = (x - hi.astype(F32)).astype(BF16)
    return hi, lo


def _split3(x):
    hi = x.astype(BF16)
    r = x - hi.astype(F32)
    mid = r.astype(BF16)
    lo = (r - mid.astype(F32)).astype(BF16)
    return hi, mid, lo


def _dot(a, b):
    return jnp.dot(a, b, preferred_element_type=F32)


def _dot_nt(a, b):
    return lax.dot_general(a, b, (((1,), (1,)), ((), ())), preferred_element_type=F32)


def _dot_tn(a, b):
    return lax.dot_general(a, b, (((0,), (0,)), ((), ())), preferred_element_type=F32)


def _dot3(a_hi, a_lo, b_hi, b_lo):
    return _dot(a_hi, b_hi) + _dot(a_lo, b_hi) + _dot(a_hi, b_lo)


def _memkv_kernel(mem_ref, g_ref, w_ref, o_ref):
    h = _rms(mem_ref[0], g_ref[...]).astype(BF16)
    o_ref[0] = _dot(h, w_ref[...]).astype(BF16)


def _mem_kv(mem, g, w_kv):
    nb = mem.shape[0]
    return pl.pallas_call(
        _memkv_kernel,
        out_shape=jax.ShapeDtypeStruct((nb, N_MEM, 2 * MEM_DIM), BF16),
        grid=(nb,),
        in_specs=[
            pl.BlockSpec((1, N_MEM, D_MODEL), lambda b: (b, 0, 0)),
            pl.BlockSpec((1, D_MODEL), lambda b: (0, 0)),
            pl.BlockSpec((D_MODEL, 2 * MEM_DIM), lambda b: (0, 0)),
        ],
        out_specs=pl.BlockSpec((1, N_MEM, 2 * MEM_DIM), lambda b: (b, 0, 0)),
        compiler_params=_cparams(("arbitrary",)),
        name="mem_kv",
    )(mem, g, w_kv)


def _inproj_kernel(x_ref, g_ref, w_ref, o_ref):
    h = _rms(x_ref[...], g_ref[...]).astype(BF16)
    for c0, cw in PROJ_CHUNKS:
        o_ref[:, c0:c0 + cw] = _dot(h, w_ref[:, c0:c0 + cw])


def _inproj(x2, g, w_in_r, tm):
    t = x2.shape[0]
    return pl.pallas_call(
        _inproj_kernel,
        out_shape=jax.ShapeDtypeStruct((t, PROJ_DIM), F32),
        grid=(t // tm,),
        in_specs=[
            pl.BlockSpec((tm, D_MODEL), lambda i: (i, 0)),
            pl.BlockSpec((1, D_MODEL), lambda i: (0, 0)),
            pl.BlockSpec((D_MODEL, PROJ_DIM), lambda i: (0, 0)),
        ],
        out_specs=pl.BlockSpec((tm, PROJ_DIM), lambda i: (i, 0)),
        compiler_params=_cparams(("arbitrary",)),
        name="inproj",
    )(x2, g, w_in_r)


def _gla_kernel(qf, kf, vf, lf, qb, kb, vb, lb, w2f_hi, w2f_lo, bf_ref, w2b_hi, w2b_lo, bb_ref,
                of_ref, ob_ref, sf_ref, sb_ref, *, nchunk):
    @pl.when(pl.program_id(1) == 0)
    def _():
        sf_ref[...] = jnp.zeros_like(sf_ref)
        sb_ref[...] = jnp.zeros_like(sb_ref)

    c = GLA_CHUNK
    pr = 2 * c
    dirs = (0, 1)
    heads = range(GLA_HEADS)
    ks = [slice(h * GLA_DK, (h + 1) * GLA_DK) for h in heads]
    vs = [slice(h * GLA_DV, (h + 1) * GLA_DV) for h in heads]
    pairs = [slice(p * pr, (p + 1) * pr) for p in range(nchunk // 2)]
    chunks = [slice(j * c, (j + 1) * c) for j in range(nchunk)]
    q = (qf[...], qb[...])
    k = (kf[...], kb[...])
    v16 = (vf[...].astype(BF16), vb[...].astype(BF16))
    st_refs = (sf_ref, sb_ref)

    lr = [_split2(l[...]) for l in (lf, lb)]
    w2 = ((w2f_hi[...], w2f_lo[...]), (w2b_hi[...], w2b_lo[...]))
    bias = (bf_ref[...], bb_ref[...])
    z = [_dot3(lr[d][0], lr[d][1], w2[d][0], w2[d][1]) + bias[d] for d in dirs]
    log_a = [(jnp.minimum(z[d], 0.0) - jnp.log(1.0 + jnp.exp(-jnp.abs(z[d])))) * (1.0 / GLA_GATE_TAU)
             for d in dirs]

    row = lax.broadcasted_iota(I32, (c, c), 0)
    col = lax.broadcasted_iota(I32, (c, c), 1)
    tri = (jnp.where(col <= row, 1.0, 0.0).astype(BF16), jnp.where(col >= row, 1.0, 0.0).astype(BF16))
    la = [_split2(log_a[d]) for d in dirs]
    bcum_c = [[_dot(tri[d], la[d][0][r]) + _dot(tri[d], la[d][1][r]) for r in chunks] for d in dirs]
    blast_c = [[bc[c - 1:c, :] if d == 0 else bc[0:1, :] for bc in bcum_c[d]] for d in dirs]
    bcum = [jnp.concatenate(bcum_c[d], axis=0) for d in dirs]
    blast = [jnp.concatenate([jnp.broadcast_to(b, (c, GLA_QK)) for b in blast_c[d]], axis=0) for d in dirs]
    q_s = [(q[d] * jnp.exp(bcum[d])).astype(BF16) for d in dirs]
    k_s = [(k[d] * jnp.exp(-bcum[d])).astype(BF16) for d in dirs]
    k_end = [(k[d] * jnp.exp(blast[d] - bcum[d])).astype(BF16) for d in dirs]

    prow = lax.broadcasted_iota(I32, (pr, pr), 0)
    pcol = lax.broadcasted_iota(I32, (pr, pr), 1)
    in_chunk = prow & (c - 1)
    keep = ((prow - pcol).astype(jnp.uint32) <= in_chunk.astype(jnp.uint32),
            (pcol - prow).astype(jnp.uint32) <= (c - 1 - in_chunk).astype(jnp.uint32))
    attn = [[[jnp.where(keep[d], _dot_nt(q_s[d][r, ks[h]], k_s[d][r, ks[h]]), 0.0).astype(BF16)
              for r in pairs] for h in heads] for d in dirs]
    kv = [[[_dot_tn(v16[d][r, vs[h]], k_end[d][r, ks[h]]) for r in chunks] for h in heads]
          for d in dirs]
    intra = [[jnp.concatenate([_dot(attn[d][h][p], v16[d][r, vs[h]]) for p, r in enumerate(pairs)], axis=0)
              for h in heads] for d in dirs]

    st = [[st_refs[d][h] for h in heads] for d in dirs]
    inter = [[[None] * nchunk for _ in heads] for _ in dirs]
    for step in range(nchunk):
        for d in dirs:
            j = step if d == 0 else nchunk - 1 - step
            dec = jnp.exp(blast_c[d][j])
            for h in heads:
                inter[d][h][j] = _dot_nt(q_s[d][chunks[j], ks[h]], st[d][h].astype(BF16))
                st[d][h] = st[d][h] * dec[:, ks[h]] + kv[d][h][j]
    for d, o_ref in zip(dirs, (of_ref, ob_ref)):
        for h in heads:
            st_refs[d][h] = st[d][h]
        o_ref[...] = jnp.concatenate([intra[d][h] + jnp.concatenate(inter[d][h], axis=0) for h in heads],
                                     axis=1)


def _gla(proj, w2f, bf, w2b, bb, nbatch, seq, tb):
    t = proj.shape[0]
    ns = seq // tb
    w_q, w_v = GLA_QK, GLA_V

    def fwd(cb):
        return lambda b, i: (b * ns + i, cb)

    def bwd(cb):
        return lambda b, i: (b * ns + ns - 1 - i, cb)

    const = lambda b, i: (0, 0)
    in_specs = [
        pl.BlockSpec((tb, w_q), fwd(COL_Q // w_q)),
        pl.BlockSpec((tb, w_q), fwd(COL_K // w_q)),
        pl.BlockSpec((tb, w_v), fwd(COL_V // w_v)),
        pl.BlockSpec((tb, LR_WIDTH), fwd(COL_LR // LR_WIDTH)),
        pl.BlockSpec((tb, w_q), bwd(COL_Q // w_q)),
        pl.BlockSpec((tb, w_q), bwd(COL_K // w_q)),
        pl.BlockSpec((tb, w_v), bwd(COL_V // w_v)),
        pl.BlockSpec((tb, LR_WIDTH), bwd(COL_LR // LR_WIDTH)),
        pl.BlockSpec((LR_WIDTH, GLA_QK), const),
        pl.BlockSpec((LR_WIDTH, GLA_QK), const),
        pl.BlockSpec((1, GLA_QK), const),
        pl.BlockSpec((LR_WIDTH, GLA_QK), const),
        pl.BlockSpec((LR_WIDTH, GLA_QK), const),
        pl.BlockSpec((1, GLA_QK), const),
    ]
    out_specs = [
        pl.BlockSpec((tb, w_v), lambda b, i: (b * ns + i, 0)),
        pl.BlockSpec((tb, w_v), lambda b, i: (b * ns + ns - 1 - i, 0)),
    ]
    w2f_hi, w2f_lo = w2f
    w2b_hi, w2b_lo = w2b
    return pl.pallas_call(
        functools.partial(_gla_kernel, nchunk=tb // GLA_CHUNK),
        out_shape=[jax.ShapeDtypeStruct((t, w_v), F32), jax.ShapeDtypeStruct((t, w_v), F32)],
        grid=(nbatch, ns),
        in_specs=in_specs,
        out_specs=out_specs,
        scratch_shapes=[pltpu.VMEM((GLA_HEADS, GLA_DV, GLA_DK), F32),
                        pltpu.VMEM((GLA_HEADS, GLA_DV, GLA_DK), F32)],
        compiler_params=_cparams(("arbitrary", "arbitrary")),
        name="gla",
    )(proj, proj, proj, proj, proj, proj, proj, proj, w2f_hi, w2f_lo, bf, w2b_hi, w2b_lo, bb)


def _mix_kernel(x_ref, gate_ref, pool_ref, pprev_ref, pnext_ref, r_ref, qm_ref, of_ref, ob_ref,
                kv_ref, poolw_ref, pscale_ref, gnorm_ref, wup_pool_ref, wup_gla_ref, wup_mem_ref,
                wout_ref, gffn_ref, wr_hi_ref, wr_lo_ref,
                x1_ref, h2_ref, aff_ref, pp_ref, band_ref, *, tm, seq):
    s0 = (pl.program_id(0) % (seq // tm)) * tm
    hal = POOL_PAD
    win = MIX_SUB + 2 * hal

    @pl.when(pl.program_id(0) == 0)
    def _():
        t_i = lax.broadcasted_iota(I32, (MIX_SUB, win), 0)
        j_i = lax.broadcasted_iota(I32, (MIX_SUB, win), 1)
        for g, w in enumerate(POOL_WINDOWS):
            inside = (j_i - hal - t_i + w // 2).astype(jnp.uint32) < jnp.uint32(w)
            band_ref[g] = jnp.where(inside, 1.0, 0.0).astype(BF16)

    pp_ref[0:hal, :] = jnp.where(s0 > 0, pprev_ref[...], 0.0).astype(BF16)
    pp_ref[hal:hal + tm, :] = pool_ref[...].astype(BF16)
    pp_ref[hal + tm:hal + tm + hal, :] = jnp.where(s0 + tm < seq, pnext_ref[...], 0.0).astype(BF16)
    subs = [slice(r0, r0 + MIX_SUB) for r0 in range(0, tm, MIX_SUB)]
    kv = kv_ref[0]

    groups = [slice(g * POOL_GROUP_DIM, (g + 1) * POOL_GROUP_DIM) for g in range(len(POOL_WINDOWS))]
    mheads = [slice(h * MEM_HEAD_DIM, (h + 1) * MEM_HEAD_DIM) for h in range(MEM_HEADS)]

    tots = [[_dot(band_ref[g], pp_ref[rs.start:rs.start + win, cs]) for g, cs in enumerate(groups)] for rs in subs]
    qms = [qm_ref[rs, :].astype(BF16) for rs in subs]
    scores = [[_dot_nt(qm[:, hs], kv[:, hs]) * (MEM_HEAD_DIM ** -0.5) for hs in mheads] for qm in qms]

    def pool_maps(rs, tot):
        pos = s0 + rs.start + lax.broadcasted_iota(I32, (MIX_SUB, 1), 0)
        out = []
        for g, w in enumerate(POOL_WINDOWS):
            cnt = (jnp.minimum(pos + (w - w // 2), seq) - jnp.maximum(pos - w // 2, 0)).astype(F32)
            dlt = (tot[g] / cnt - pool_ref[rs, groups[g]]).astype(BF16)
            out.append(_dot(dlt, poolw_ref[g]))
        return (jnp.concatenate(out, axis=1) * pscale_ref[...]).astype(BF16)

    def mem_values(sc):
        om = []
        for h, s in enumerate(sc):
            e = jnp.exp(s - jnp.max(s, axis=-1, keepdims=True))
            p = (e / jnp.sum(e, axis=-1, keepdims=True)).astype(BF16)
            om.append(_dot(p, kv[:, MEM_DIM + h * MEM_HEAD_DIM:MEM_DIM + (h + 1) * MEM_HEAD_DIM]))
        return jnp.concatenate(om, axis=1).astype(BF16)

    def gla_out(rs):
        o = of_ref[rs, :] + ob_ref[rs, :]
        on = []
        for h in range(GLA_HEADS):
            oh = o[:, h * GLA_DV:(h + 1) * GLA_DV]
            on.append(oh * lax.rsqrt(jnp.mean(oh * oh, axis=-1, keepdims=True) + EPS))
        r = r_ref[rs, :]
        silu = r + jnp.tanh(r) * r
        return (jnp.concatenate(on, axis=1) * gnorm_ref[...] * silu).astype(BF16)

    y_pool = [pool_maps(rs, tot) for rs, tot in zip(subs, tots)]
    y_mem = [mem_values(sc) for sc in scores]
    y_gla = [gla_out(rs) for rs in subs]

    ups = [(_dot(yp, wup_pool_ref[...]), _dot(yg, wup_gla_ref[...]), _dot(ym, wup_mem_ref[...]))
           for yp, yg, ym in zip(y_pool, y_gla, y_mem)]

    def merge(rs, ups):
        gated = [y + jnp.tanh(gate_ref[rs, j * D_MODEL:(j + 1) * D_MODEL]) * y for j, y in enumerate(ups)]
        merged = (gated[0] + gated[1] + gated[2]).astype(BF16)
        x1 = x_ref[rs, :] + _dot(merged, wout_ref[...])
        x1_ref[rs, :] = x1
        return x1

    def route(rs, x1):
        h2 = _rms(x1, gffn_ref[...])
        h_hi, h_lo = _split2(h2)
        h2_ref[rs, :] = h_hi
        wr_hi, wr_lo = wr_hi_ref[...], wr_lo_ref[...]
        logits = _dot_nt(wr_hi, h_hi) + _dot_nt(wr_hi, h_lo) + _dot_nt(wr_lo, h_hi)
        e = jnp.exp(logits - jnp.max(logits, axis=0, keepdims=True))
        aff_ref[:, rs] = e / jnp.sum(e, axis=0, keepdims=True)

    x1s = [merge(rs, u) for rs, u in zip(subs, ups)]
    for rs, x1 in zip(subs, x1s):
        route(rs, x1)


def _mix(x2, proj, o_f, o_b, kv, wts, nbatch, seq, tm):
    t = x2.shape[0]
    ns = seq // tm
    h8 = tm // POOL_PAD
    n8 = t // POOL_PAD
    const2 = lambda i: (0, 0)
    in_specs = [
        pl.BlockSpec((tm, D_MODEL), lambda i: (i, 0)),
        pl.BlockSpec((tm, N_BRANCH * D_MODEL), lambda i: (i, COL_GATE // (N_BRANCH * D_MODEL))),
        pl.BlockSpec((tm, POOL_DIM), lambda i: (i, COL_POOL // POOL_DIM)),
        pl.BlockSpec((POOL_PAD, POOL_DIM), lambda i: (jnp.maximum(i * h8 - 1, 0), COL_POOL // POOL_DIM)),
        pl.BlockSpec((POOL_PAD, POOL_DIM), lambda i: (jnp.minimum((i + 1) * h8, n8 - 1), COL_POOL // POOL_DIM)),
        pl.BlockSpec((tm, GLA_V), lambda i: (i, COL_R // GLA_V)),
        pl.BlockSpec((tm, MEM_DIM), lambda i: (i, COL_QM // MEM_DIM)),
        pl.BlockSpec((tm, GLA_V), lambda i: (i, 0)),
        pl.BlockSpec((tm, GLA_V), lambda i: (i, 0)),
        pl.BlockSpec((1, N_MEM, 2 * MEM_DIM), lambda i: (i // ns, 0, 0)),
        pl.BlockSpec((len(POOL_WINDOWS), POOL_GROUP_DIM, POOL_GROUP_DIM), lambda i: (0, 0, 0)),
        pl.BlockSpec((1, POOL_DIM), const2),
        pl.BlockSpec((1, GLA_V), const2),
        pl.BlockSpec((POOL_DIM, D_MODEL), const2),
        pl.BlockSpec((GLA_V, D_MODEL), const2),
        pl.BlockSpec((MEM_DIM, D_MODEL), const2),
        pl.BlockSpec((D_MODEL, D_MODEL), const2),
        pl.BlockSpec((1, D_MODEL), const2),
        pl.BlockSpec((N_EXPERTS, D_MODEL), const2),
        pl.BlockSpec((N_EXPERTS, D_MODEL), const2),
    ]
    out_specs = [
        pl.BlockSpec((tm, D_MODEL), lambda i: (i, 0)),
        pl.BlockSpec((tm, D_MODEL), lambda i: (i, 0)),
        pl.BlockSpec((N_EXPERTS, tm), lambda i: (0, i)),
    ]
    return pl.pallas_call(
        functools.partial(_mix_kernel, tm=tm, seq=seq),
        out_shape=[jax.ShapeDtypeStruct((t, D_MODEL), F32), jax.ShapeDtypeStruct((t, D_MODEL), BF16),
                   jax.ShapeDtypeStruct((N_EXPERTS, t), F32)],
        grid=(t // tm,),
        in_specs=in_specs,
        out_specs=out_specs,
        scratch_shapes=[pltpu.VMEM((tm + 2 * POOL_PAD, POOL_DIM), BF16),
                        pltpu.VMEM((len(POOL_WINDOWS), MIX_SUB, MIX_SUB + 2 * POOL_PAD), BF16)],
        compiler_params=_cparams(("arbitrary",)),
        name="mix",
    )(x2, proj, proj, proj, proj, proj, proj, o_f, o_b, kv, *wts)


def _route_kernel(abt_ref, rank_ref, slo_ref, cnt_ref, thr_ref, *, nt, td, cap):
    ne = N_EXPERTS
    aff_all = abt_ref[...]

    def bit_step(i, thr):
        cand = thr | (jnp.int32(1) << (30 - i))
        hit = jnp.where(aff_all >= lax.bitcast_convert_type(cand, F32), 1.0, 0.0)
        cnt = jnp.sum(jnp.sum(hit, axis=1, keepdims=True), axis=2, keepdims=True)
        return jnp.where(cnt >= cap, cand, thr)

    thr = lax.fori_loop(0, 31, bit_step, jnp.zeros((ne, 1, 1), I32))
    thr_ref[...] = jnp.broadcast_to(lax.bitcast_convert_type(thr, F32), thr_ref.shape)

    t_row = lax.broadcasted_iota(I32, (td, td), 0)
    t_col = lax.broadcasted_iota(I32, (td, td), 1)
    upto = jnp.where(t_row <= t_col, 1.0, 0.0).astype(BF16)
    ones_t = jnp.ones((td, LANES), BF16)
    i_row = lax.broadcasted_iota(I32, (nt, nt), 0)
    i_col = lax.broadcasted_iota(I32, (nt, nt), 1)
    before = jnp.where(i_col < i_row, 1.0, 0.0).astype(BF16)

    def expert(e, carry):
        thr_e = thr_ref[e][0:1, :]
        aff = abt_ref[e]
        gt = jnp.where(aff > thr_e, 1.0, 0.0)
        eq = jnp.where(aff == thr_e, 1.0, 0.0)
        n_gt = jnp.sum(jnp.sum(gt, axis=0, keepdims=True), axis=1, keepdims=True)
        need = cap - n_gt
        eq16 = eq.astype(BF16)
        tot_eq = _dot(eq16, ones_t)
        ex_eq = _dot(before, tot_eq.astype(BF16))
        rel_eq = _dot(eq16, upto)
        sel = gt + eq * jnp.where(ex_eq[:, 0:1] + rel_eq <= need, 1.0, 0.0)
        rank_ref[e] = jnp.where(sel > 0.0, _dot(sel.astype(BF16), upto) - 1.0, -1.0)
        cnt = _dot(sel.astype(BF16), ones_t)
        pad = jnp.floor((cnt + float(ROW_ALIGN - 1)) * (1.0 / ROW_ALIGN)) * float(ROW_ALIGN)
        slo_ref[e] = _dot(before, pad.astype(BF16)).astype(I32)
        cnt_ref[e] = cnt.astype(I32)
        return carry

    lax.fori_loop(0, ne, expert, 0)


def _route(a_bt, cap):
    ne, nt, td = a_bt.shape
    return pl.pallas_call(
        functools.partial(_route_kernel, nt=nt, td=td, cap=cap),
        out_shape=[jax.ShapeDtypeStruct((ne, nt, td), F32), jax.ShapeDtypeStruct((ne, nt, LANES), I32),
                   jax.ShapeDtypeStruct((ne, nt, LANES), I32)],
        grid=(1,),
        in_specs=[pl.BlockSpec((ne, nt, td), lambda i: (0, 0, 0))],
        out_specs=[pl.BlockSpec((ne, nt, td), lambda i: (0, 0, 0)), pl.BlockSpec((ne, nt, LANES), lambda i: (0, 0, 0)),
                   pl.BlockSpec((ne, nt, LANES), lambda i: (0, 0, 0))],
        scratch_shapes=[pltpu.VMEM((ne, 8, td), F32)],
        compiler_params=_cparams(("arbitrary",)),
        name="route",
    )(a_bt)


def _segment_blocks(rank, chunk):
    td = rank.shape[1]
    r_iota = lax.broadcasted_iota(I32, (SEG_ROWS, td), 0).astype(F32) + jnp.asarray(chunk * SEG_ROWS, F32)
    return [jnp.where(rank[e:e + 1, :] == r_iota, 1.0, 0.0) for e in range(N_EXPERTS)]


def _segment_onehot(rank, chunk):
    return jnp.concatenate(_segment_blocks(rank, chunk), axis=0).astype(BF16)


def _dispatch_kernel(slo_ref, cnt_ref, vend_ref, rank_ref, h2_ref, aff_ref, xe_hbm, ge_hbm,
                     xbuf, gbuf, xov, gov, sem_x, sem_g, sem_ov, *, nt, cpad):
    i = pl.program_id(0)
    slot = i % 2

    @pl.when(i == 0)
    def _():
        xov[...] = jnp.zeros_like(xov)
        gov[...] = jnp.zeros_like(gov)

        def fill(e, first, rows, wait):
            row0 = pl.multiple_of(first, ROW_ALIGN)
            cx = pltpu.make_async_copy(xov.at[pl.ds(0, rows), :], xe_hbm.at[pl.ds(row0, rows), :], sem_ov)
            cg = pltpu.make_async_copy(gov.at[pl.ds(0, rows), :], ge_hbm.at[pl.ds(row0, rows), :], sem_ov)
            if wait:
                cx.wait()
                cg.wait()
            else:
                cx.start()
                cg.start()

        for wait in (False, True):
            for e in range(N_EXPERTS):
                tail0 = e * cpad + vend_ref[e]
                n_big = (cpad - vend_ref[e]) // SEG_ROWS
                n_small = ((cpad - vend_ref[e]) % SEG_ROWS) // ROW_ALIGN

                def big(k, c, tail0=tail0, e=e, wait=wait):
                    fill(e, tail0 + k * SEG_ROWS, SEG_ROWS, wait)
                    return c

                def small(k, c, tail0=tail0, n_big=n_big, e=e, wait=wait):
                    fill(e, tail0 + n_big * SEG_ROWS + k * ROW_ALIGN, ROW_ALIGN, wait)
                    return c

                lax.fori_loop(0, n_big, big, 0)
                lax.fori_loop(0, n_small, small, 0)

    def seg_copies(step, buf_slot, e):
        row0 = pl.multiple_of(slo_ref[step, e], ROW_ALIGN) + e * cpad
        src = pl.ds(e * SEG_ROWS, SEG_ROWS)
        return (pltpu.make_async_copy(xbuf.at[buf_slot, src, :], xe_hbm.at[pl.ds(row0, SEG_ROWS), :], sem_x),
                pltpu.make_async_copy(gbuf.at[buf_slot, src, :], ge_hbm.at[pl.ds(row0, SEG_ROWS), :], sem_g))

    def wait_step(step, buf_slot):
        for e in range(N_EXPERTS):
            cx, cg = seg_copies(step, buf_slot, e)
            cx.wait()
            cg.wait()

    @pl.when(i > 0)
    def _():
        wait_step(i - 1, 1 - slot)

    rank = rank_ref[...]
    h2 = h2_ref[...]
    aff = aff_ref[...]

    def rows_for(chunk):
        blocks = _segment_blocks(rank, chunk)
        xr = _dot(jnp.concatenate(blocks, axis=0).astype(BF16), h2).astype(BF16)
        gs = [jnp.broadcast_to(jnp.sum(blocks[e] * aff[e:e + 1, :], axis=1, keepdims=True), (SEG_ROWS, LANES))
              for e in range(N_EXPERTS)]
        return xr, jnp.concatenate(gs, axis=0)

    xr, gr = rows_for(0)
    xbuf[slot] = xr
    gbuf[slot] = gr
    for e in range(N_EXPERTS):
        cx, cg = seg_copies(i, slot, e)
        cx.start()
        cg.start()

    cmax = cnt_ref[i, 0]
    for e in range(1, N_EXPERTS):
        cmax = jnp.maximum(cmax, cnt_ref[i, e])

    def extra_chunk(chunk, carry):
        xo, go = rows_for(chunk)
        xov[...] = xo
        gov[...] = go
        for e in range(N_EXPERTS):
            @pl.when(cnt_ref[i, e] > chunk * SEG_ROWS)
            def _(e=e):
                row0 = pl.multiple_of(slo_ref[i, e] + chunk * SEG_ROWS, ROW_ALIGN) + e * cpad
                src = pl.ds(e * SEG_ROWS, SEG_ROWS)
                cx = pltpu.make_async_copy(xov.at[src, :], xe_hbm.at[pl.ds(row0, SEG_ROWS), :], sem_ov)
                cg = pltpu.make_async_copy(gov.at[src, :], ge_hbm.at[pl.ds(row0, SEG_ROWS), :], sem_ov)
                cx.start()
                cg.start()
                cx.wait()
                cg.wait()
        return carry

    lax.fori_loop(1, (cmax + SEG_ROWS - 1) // SEG_ROWS, extra_chunk, 0)

    @pl.when(i == nt - 1)
    def _():
        wait_step(i, slot)


def _dispatch(slo_t, cnt_t, vend, rank2, h2, aff, cpad):
    ne, t = rank2.shape
    td = DISPATCH_TILE
    nt = t // td
    grid_spec = pltpu.PrefetchScalarGridSpec(
        num_scalar_prefetch=3,
        grid=(nt,),
        in_specs=[
            pl.BlockSpec((ne, td), lambda i, s, c, v: (0, i)),
            pl.BlockSpec((td, D_MODEL), lambda i, s, c, v: (i, 0)),
            pl.BlockSpec((ne, td), lambda i, s, c, v: (0, i)),
        ],
        out_specs=[pl.BlockSpec(memory_space=pl.ANY), pl.BlockSpec(memory_space=pl.ANY)],
        scratch_shapes=[pltpu.VMEM((2, ne * SEG_ROWS, D_MODEL), BF16), pltpu.VMEM((2, ne * SEG_ROWS, LANES), F32),
                        pltpu.VMEM((ne * SEG_ROWS, D_MODEL), BF16), pltpu.VMEM((ne * SEG_ROWS, LANES), F32),
                        pltpu.SemaphoreType.DMA, pltpu.SemaphoreType.DMA, pltpu.SemaphoreType.DMA],
    )
    return pl.pallas_call(
        functools.partial(_dispatch_kernel, nt=nt, cpad=cpad),
        out_shape=[jax.ShapeDtypeStruct((ne * cpad, D_MODEL), BF16), jax.ShapeDtypeStruct((ne * cpad, LANES), F32)],
        grid_spec=grid_spec,
        compiler_params=_cparams(("arbitrary",)),
        name="dispatch",
    )(slo_t, cnt_t, vend, rank2, h2, aff)


def _ffn_kernel(vend_ref, x_ref, g_ref, wg_ref, wu_ref, wd_ref, o_ref, *, ft):
    e = pl.program_id(0)
    row0 = pl.program_id(1) * ft
    vend = vend_ref[e]

    @pl.when(row0 < vend)
    def _():
        valid = row0 + lax.broadcasted_iota(I32, (ft, 1), 0) < vend
        x = jnp.where(valid, x_ref[...], jnp.zeros((), BF16))
        hg = _dot(x, wg_ref[0])
        hu = _dot(x, wu_ref[0])
        hid = ((hg + jnp.tanh(hg) * hg) * hu).astype(BF16)
        gate = jnp.where(valid, g_ref[:, 0:1], 0.0)
        o_ref[...] = (_dot(hid, wd_ref[0]) * gate).astype(BF16)

    @pl.when(row0 >= vend)
    def _():
        o_ref[...] = jnp.zeros_like(o_ref)


def _ffn(vend, xe, ge, w_gate, w_up, w_down, cpad, ft):
    ne = N_EXPERTS
    nj = cpad // ft
    wspec = pl.BlockSpec((1, D_MODEL, D_MODEL), lambda e, j, v: (e, 0, 0))
    grid_spec = pltpu.PrefetchScalarGridSpec(
        num_scalar_prefetch=1,
        grid=(ne, nj),
        in_specs=[
            pl.BlockSpec((ft, D_MODEL), lambda e, j, v: (e * nj + j, 0)),
            pl.BlockSpec((ft, LANES), lambda e, j, v: (e * nj + j, 0)),
            wspec, wspec, wspec,
        ],
        out_specs=pl.BlockSpec((ft, D_MODEL), lambda e, j, v: (e * nj + j, 0)),
    )
    return pl.pallas_call(
        functools.partial(_ffn_kernel, ft=ft),
        out_shape=jax.ShapeDtypeStruct((ne * cpad, D_MODEL), BF16),
        grid_spec=grid_spec,
        compiler_params=_cparams(("arbitrary", "arbitrary")),
        name="ffn",
    )(vend, xe, ge, w_gate, w_up, w_down)


def _combine_kernel(slo_ref, cnt_ref, rank_ref, x1_ref, gfin_ref, ye_hbm, y_ref, ybuf, yov, acc_ref,
                    sem, sem_ov, *, nsteps, subs, cpad):
    i = pl.program_id(0)
    slot = i % 2
    td = DISPATCH_TILE

    def seg_copy(step, buf_slot, u, e):
        row0 = pl.multiple_of(slo_ref[step * subs + u, e], ROW_ALIGN) + e * cpad
        return pltpu.make_async_copy(ye_hbm.at[pl.ds(row0, SEG_ROWS), :],
                                     ybuf.at[buf_slot, u, pl.ds(e * SEG_ROWS, SEG_ROWS), :], sem.at[buf_slot])

    def fetch(step, buf_slot):
        for u in range(subs):
            for e in range(N_EXPERTS):
                seg_copy(step, buf_slot, u, e).start()

    @pl.when(i == 0)
    def _():
        fetch(0, 0)

    @pl.when(i + 1 < nsteps)
    def _():
        fetch(i + 1, 1 - slot)

    for u in range(subs):
        for e in range(N_EXPERTS):
            seg_copy(i, slot, u, e).wait()

    ranks = [rank_ref[:, u * td:(u + 1) * td] for u in range(subs)]
    onehots = [_segment_onehot(r, 0) for r in ranks]
    rows = [ybuf[slot, u] for u in range(subs)]
    for u in range(subs):
        acc_ref[u * td:(u + 1) * td, :] = x1_ref[u * td:(u + 1) * td, :] + _dot_tn(onehots[u], rows[u])

    for u in range(subs):
        tile = i * subs + u
        cmax = cnt_ref[tile, 0]
        for e in range(1, N_EXPERTS):
            cmax = jnp.maximum(cmax, cnt_ref[tile, e])

        def extra_chunk(chunk, carry, u=u, tile=tile):
            for e in range(N_EXPERTS):
                dst = pl.ds(e * SEG_ROWS, SEG_ROWS)

                @pl.when(cnt_ref[tile, e] > chunk * SEG_ROWS)
                def _(e=e, dst=dst):
                    row0 = pl.multiple_of(slo_ref[tile, e] + chunk * SEG_ROWS, ROW_ALIGN) + e * cpad
                    cp = pltpu.make_async_copy(ye_hbm.at[pl.ds(row0, SEG_ROWS), :], yov.at[dst, :], sem_ov)
                    cp.start()
                    cp.wait()

                @pl.when(cnt_ref[tile, e] <= chunk * SEG_ROWS)
                def _(dst=dst):
                    yov[dst, :] = jnp.zeros((SEG_ROWS, D_MODEL), BF16)
            acc_ref[u * td:(u + 1) * td, :] += _dot_tn(_segment_onehot(ranks[u], chunk), yov[...])
            return carry

        lax.fori_loop(1, (cmax + SEG_ROWS - 1) // SEG_ROWS, extra_chunk, 0)

    y_ref[...] = _rms(acc_ref[...], gfin_ref[...])


def _combine(slo_t, cnt_t, rank2, x1, g_final, ye, cpad):
    ne, t = rank2.shape
    nt = t // DISPATCH_TILE
    subs = 2 if nt % 2 == 0 else 1
    tc = subs * DISPATCH_TILE
    grid_spec = pltpu.PrefetchScalarGridSpec(
        num_scalar_prefetch=2,
        grid=(nt // subs,),
        in_specs=[
            pl.BlockSpec((ne, tc), lambda i, s, c: (0, i)),
            pl.BlockSpec((tc, D_MODEL), lambda i, s, c: (i, 0)),
            pl.BlockSpec((1, D_MODEL), lambda i, s, c: (0, 0)),
            pl.BlockSpec(memory_space=pl.ANY),
        ],
        out_specs=pl.BlockSpec((tc, D_MODEL), lambda i, s, c: (i, 0)),
        scratch_shapes=[pltpu.VMEM((2, subs, ne * SEG_ROWS, D_MODEL), BF16),
                        pltpu.VMEM((ne * SEG_ROWS, D_MODEL), BF16),
                        pltpu.VMEM((tc, D_MODEL), F32),
                        pltpu.SemaphoreType.DMA((2,)), pltpu.SemaphoreType.DMA],
    )
    return pl.pallas_call(
        functools.partial(_combine_kernel, nsteps=nt // subs, subs=subs, cpad=cpad),
        out_shape=jax.ShapeDtypeStruct((t, D_MODEL), F32),
        grid_spec=grid_spec,
        compiler_params=_cparams(("arbitrary",)),
        name="combine",
    )(slo_t, cnt_t, rank2, x1, g_final, ye)


def _prep_weights(norm_mix_g, w_in, gla_w2_f, gla_b_f, gla_w2_b, gla_b_b, gla_norm_g, pool_w, pool_scale,
                  mem_norm_g, w_mem_kv, w_up_pool, w_up_gla, w_up_mem, w_out, norm_ffn_g, w_router,
                  w_e_gate, w_e_up, w_e_down, norm_final_g):
    o_pool, o_q, o_k, o_v, o_r = 0, 512, 768, 1024, 1536
    o_lf, o_lb, o_qm, o_gate, o_end = 2048, 2064, 2080, 2592, 5664
    w = w_in[0]
    pad = jnp.zeros((D_MODEL, LR_WIDTH - 2 * GLA_GATE_RANK), F32)
    w_in_r = jnp.concatenate([w[:, o_gate:o_end] * 0.5, w[:, o_pool:o_q], w[:, o_v:o_r], w[:, o_r:o_lf] * 0.5,
                              w[:, o_qm:o_gate], w[:, o_q:o_k] * (GLA_DK ** -0.5), w[:, o_k:o_v], w[:, o_lf:o_lb],
                              w[:, o_lb:o_qm], pad], axis=1).astype(BF16)

    def pad_w2(w2, row0):
        full = jnp.zeros((LR_WIDTH, GLA_QK), F32).at[row0:row0 + GLA_GATE_RANK].set(w2)
        return _split2(full)

    row = lambda v: v.reshape(1, -1)
    wr_hi, wr_lo = _split2(w_router[0].T)
    return dict(
        norm_mix_g=row(norm_mix_g[0]), w_in_r=w_in_r,
        w2f=pad_w2(gla_w2_f[0], 0), bf=row(gla_b_f[0]),
        w2b=pad_w2(gla_w2_b[0], GLA_GATE_RANK), bb=row(gla_b_b[0]),
        mem_norm_g=row(mem_norm_g[0]), w_mem_kv=w_mem_kv[0].astype(BF16),
        mix=(pool_w[0].astype(BF16), row(pool_scale[0]), row(gla_norm_g[0]), w_up_pool[0].astype(BF16),
             w_up_gla[0].astype(BF16), w_up_mem[0].astype(BF16), (w_out[0] * 0.5).astype(BF16),
             row(norm_ffn_g[0]), wr_hi, wr_lo),
        w_e_gate=(w_e_gate[0] * 0.5).astype(BF16), w_e_up=w_e_up[0].astype(BF16), w_e_down=w_e_down[0].astype(BF16),
        norm_final_g=row(norm_final_g),
    )


def _pick(n, pref):
    for c in pref:
        if n % c == 0:
            return c
    raise ValueError(f"no tile for {n}")


def _trunk(x, mem, p):
    nbatch, seq, _ = x.shape
    t = nbatch * seq
    x2 = x.reshape(t, D_MODEL)
    tm = _pick(seq, (256, 128))
    tb = _pick(seq, (512, 256, 128))
    kv = _mem_kv(mem, p["mem_norm_g"], p["w_mem_kv"])
    proj = _inproj(x2, p["norm_mix_g"], p["w_in_r"], tm)
    o_f, o_b = _gla(proj, p["w2f"], p["bf"], p["w2b"], p["bb"], nbatch, seq, tb)
    x1, h2, aff = _mix(x2, proj, o_f, o_b, kv, p["mix"], nbatch, seq, _pick(seq, (2 * MIX_SUB, MIX_SUB)))
    cap = max(1, min(t, EC_CAPACITY_FACTOR * t // N_EXPERTS))
    nt = t // DISPATCH_TILE
    rank, slo, cnt = _route(aff.reshape(N_EXPERTS, nt, DISPATCH_TILE), cap)
    rank2 = rank.reshape(N_EXPERTS, t)
    slo_t = slo[:, :, 0].T
    cnt_t = cnt[:, :, 0].T
    last_chunks = jnp.maximum(-(-cnt_t[nt - 1] // SEG_ROWS), 1)
    vend = slo_t[nt - 1] + last_chunks * SEG_ROWS
    cpad = -(-(cap + (ROW_ALIGN - 1) * nt + SEG_ROWS) // FFN_TILE) * FFN_TILE
    xe, ge = _dispatch(slo_t, cnt_t, vend, rank2, h2, aff, cpad)
    ye = _ffn(vend, xe, ge, p["w_e_gate"], p["w_e_up"], p["w_e_down"], cpad, FFN_TILE)
    y = _combine(slo_t, cnt_t, rank2, x1, p["norm_final_g"], ye, cpad)
    return y.reshape(nbatch, seq, D_MODEL)


def kernel(x_prompt, x_sample, mem_prompt, mem_sample, norm_mix_g, w_in, gla_w2_f, gla_b_f, gla_w2_b, gla_b_b, gla_norm_g, pool_w, pool_scale, mem_norm_g, w_mem_kv, w_up_pool, w_up_gla, w_up_mem, w_out, norm_ffn_g, w_router, w_e_gate, w_e_up, w_e_down, norm_final_g):
    p = _prep_weights(norm_mix_g, w_in, gla_w2_f, gla_b_f, gla_w2_b, gla_b_b, gla_norm_g, pool_w, pool_scale,
                      mem_norm_g, w_mem_kv, w_up_pool, w_up_gla, w_up_mem, w_out, norm_ffn_g, w_router,
                      w_e_gate, w_e_up, w_e_down, norm_final_g)
    return (_trunk(x_prompt, mem_prompt, p), _trunk(x_sample, mem_sample, p))
```

```python
import functools

import jax
import jax.numpy as jnp
from jax import lax
from jax.experimental import pallas as pl
from jax.experimental.pallas import tpu as pltpu

F32 = jnp.float32
BF16 = jnp.bfloat16
I32 = jnp.int32

D_MODEL = 1024
N_MEM = 256
POOL_WINDOWS = (2, 4, 8, 16)
POOL_GROUP_DIM = 128
POOL_DIM = 512
POOL_PAD = 64
GLA_HEADS = 4
GLA_DK = 64
GLA_DV = 128
GLA_QK = 256
GLA_V = 512
GLA_GATE_RANK = 16
GLA_GATE_TAU = 16.0
GLA_CHUNK = 64
MEM_HEADS = 4
MEM_HEAD_DIM = 128
MEM_DIM = 512
N_BRANCH = 3
N_EXPERTS = 16
EC_CAPACITY_FACTOR = 2
EPS = 1e-6

LANES = 128
MXU_DIM = 256

COL_GATE = 0
COL_POOL = COL_GATE + N_BRANCH * D_MODEL
COL_V = COL_POOL + POOL_DIM
COL_R = COL_V + GLA_V
COL_QM = COL_R + GLA_V
COL_Q = COL_QM + MEM_DIM
COL_K = COL_Q + GLA_QK
COL_LR = COL_K + GLA_QK
LR_WIDTH = MXU_DIM
PROJ_DIM = COL_LR + LR_WIDTH
PROJ_CHUNKS = ((0, 1536), (1536, 1536), (3072, 1536), (4608, 1280))

MIX_SUB = 256
DISPATCH_TILE = 256
SEG_ROWS = 48
ROW_ALIGN = 16
FFN_TILE = 512

VMEM_LIMIT = 56 * 1024 * 1024


def _cparams(semantics):
    return pltpu.CompilerParams(dimension_semantics=semantics, vmem_limit_bytes=VMEM_LIMIT)


def _rms(x, g):
    return x * lax.rsqrt(jnp.mean(x * x, axis=-1, keepdims=True) + EPS) * g


def _split2(x):
    hi = x.astype(BF16)
    lo = (x - hi.astype(F32)).astype(BF16)
    return hi, lo


def _split3(x):
    hi = x.astype(BF16)
    r = x - hi.astype(F32)
    mid = r.astype(BF16)
    lo = (r - mid.astype(F32)).astype(BF16)
    return hi, mid, lo


def _dot(a, b):
    return jnp.dot(a, b, preferred_element_type=F32)


def _dot_nt(a, b):
    return lax.dot_general(a, b, (((1,), (1,)), ((), ())), preferred_element_type=F32)


def _dot_tn(a, b):
    return lax.dot_general(a, b, (((0,), (0,)), ((), ())), preferred_element_type=F32)


def _dot3(a_hi, a_lo, b_hi, b_lo):
    return _dot(a_hi, b_hi) + _dot(a_lo, b_hi) + _dot(a_hi, b_lo)


def _memkv_kernel(mem_ref, g_ref, w_ref, o_ref):
    h = _rms(mem_ref[0], g_ref[...]).astype(BF16)
    o_ref[0] = _dot(h, w_ref[...]).astype(BF16)


def _mem_kv(mem, g, w_kv):
    nb = mem.shape[0]
    return pl.pallas_call(
        _memkv_kernel,
        out_shape=jax.ShapeDtypeStruct((nb, N_MEM, 2 * MEM_DIM), BF16),
        grid=(nb,),
        in_specs=[
            pl.BlockSpec((1, N_MEM, D_MODEL), lambda b: (b, 0, 0)),
            pl.BlockSpec((1, D_MODEL), lambda b: (0, 0)),
            pl.BlockSpec((D_MODEL, 2 * MEM_DIM), lambda b: (0, 0)),
        ],
        out_specs=pl.BlockSpec((1, N_MEM, 2 * MEM_DIM), lambda b: (b, 0, 0)),
        compiler_params=_cparams(("arbitrary",)),
        name="mem_kv",
    )(mem, g, w_kv)


def _inproj_kernel(x_ref, g_ref, w_ref, o_ref):
    h = _rms(x_ref[...], g_ref[...]).astype(BF16)
    for c0, cw in PROJ_CHUNKS:
        o_ref[:, c0:c0 + cw] = _dot(h, w_ref[:, c0:c0 + cw])


def _inproj(x2, g, w_in_r, tm):
    t = x2.shape[0]
    return pl.pallas_call(
        _inproj_kernel,
        out_shape=jax.ShapeDtypeStruct((t, PROJ_DIM), F32),
        grid=(t // tm,),
        in_specs=[
            pl.BlockSpec((tm, D_MODEL), lambda i: (i, 0)),
            pl.BlockSpec((1, D_MODEL), lambda i: (0, 0)),
            pl.BlockSpec((D_MODEL, PROJ_DIM), lambda i: (0, 0), pipeline_mode=pl.Buffered(1)),
        ],
        out_specs=pl.BlockSpec((tm, PROJ_DIM), lambda i: (i, 0)),
        compiler_params=_cparams(("arbitrary",)),
        name="inproj",
    )(x2, g, w_in_r)


def _gla_kernel(qf, kf, vf, lf, qb, kb, vb, lb, w2f_hi, w2f_lo, bf_ref, w2b_hi, w2b_lo, bb_ref,
                of_ref, ob_ref, sf_ref, sb_ref, *, nchunk):
    @pl.when(pl.program_id(1) == 0)
    def _():
        sf_ref[...] = jnp.zeros_like(sf_ref)
        sb_ref[...] = jnp.zeros_like(sb_ref)

    c = GLA_CHUNK
    pr = 2 * c
    dirs = (0, 1)
    heads = range(GLA_HEADS)
    ks = [slice(h * GLA_DK, (h + 1) * GLA_DK) for h in heads]
    vs = [slice(h * GLA_DV, (h + 1) * GLA_DV) for h in heads]
    pairs = [slice(p * pr, (p + 1) * pr) for p in range(nchunk // 2)]
    chunks = [slice(j * c, (j + 1) * c) for j in range(nchunk)]
    q = (qf[...], qb[...])
    k = (kf[...], kb[...])
    v16 = (vf[...].astype(BF16), vb[...].astype(BF16))
    st_refs = (sf_ref, sb_ref)

    lr = [_split2(l[...]) for l in (lf, lb)]
    w2 = ((w2f_hi[...], w2f_lo[...]), (w2b_hi[...], w2b_lo[...]))
    bias = (bf_ref[...], bb_ref[...])
    z = [_dot3(lr[d][0], lr[d][1], w2[d][0], w2[d][1]) + bias[d] for d in dirs]
    log_a = [(jnp.minimum(z[d], 0.0) - jnp.log(1.0 + jnp.exp(-jnp.abs(z[d])))) * (1.0 / GLA_GATE_TAU)
             for d in dirs]

    row = lax.broadcasted_iota(I32, (c, c), 0)
    col = lax.broadcasted_iota(I32, (c, c), 1)
    tri = (jnp.where(col <= row, 1.0, 0.0).astype(BF16), jnp.where(col >= row, 1.0, 0.0).astype(BF16))
    la = [_split2(log_a[d]) for d in dirs]
    bcum_c = [[_dot(tri[d], la[d][0][r]) + _dot(tri[d], la[d][1][r]) for r in chunks] for d in dirs]
    blast_c = [[bc[c - 1:c, :] if d == 0 else bc[0:1, :] for bc in bcum_c[d]] for d in dirs]
    bcum = [jnp.concatenate(bcum_c[d], axis=0) for d in dirs]
    blast = [jnp.concatenate([jnp.broadcast_to(b, (c, GLA_QK)) for b in blast_c[d]], axis=0) for d in dirs]
    q_s = [(q[d] * jnp.exp(bcum[d])).astype(BF16) for d in dirs]
    k_s = [(k[d] * jnp.exp(-bcum[d])).astype(BF16) for d in dirs]
    k_end = [(k[d] * jnp.exp(blast[d] - bcum[d])).astype(BF16) for d in dirs]

    prow = lax.broadcasted_iota(I32, (pr, pr), 0)
    pcol = lax.broadcasted_iota(I32, (pr, pr), 1)
    in_chunk = prow & (c - 1)
    keep = ((prow - pcol).astype(jnp.uint32) <= in_chunk.astype(jnp.uint32),
            (pcol - prow).astype(jnp.uint32) <= (c - 1 - in_chunk).astype(jnp.uint32))
    attn = [[[jnp.where(keep[d], _dot_nt(q_s[d][r, ks[h]], k_s[d][r, ks[h]]), 0.0).astype(BF16)
              for r in pairs] for h in heads] for d in dirs]
    kv = [[[_dot_tn(v16[d][r, vs[h]], k_end[d][r, ks[h]]) for r in chunks] for h in heads]
          for d in dirs]
    intra = [[jnp.concatenate([_dot(attn[d][h][p], v16[d][r, vs[h]]) for p, r in enumerate(pairs)], axis=0)
              for h in heads] for d in dirs]

    st = [[st_refs[d][h] for h in heads] for d in dirs]
    inter = [[[None] * nchunk for _ in heads] for _ in dirs]
    for step in range(nchunk):
        for d in dirs:
            j = step if d == 0 else nchunk - 1 - step
            dec = jnp.exp(blast_c[d][j])
            for h in heads:
                inter[d][h][j] = _dot_nt(q_s[d][chunks[j], ks[h]], st[d][h].astype(BF16))
                st[d][h] = st[d][h] * dec[:, ks[h]] + kv[d][h][j]
    for d, o_ref in zip(dirs, (of_ref, ob_ref)):
        for h in heads:
            st_refs[d][h] = st[d][h]
        o_ref[...] = jnp.concatenate([intra[d][h] + jnp.concatenate(inter[d][h], axis=0) for h in heads],
                                     axis=1)


def _gla(proj, w2f, bf, w2b, bb, nbatch, seq, tb):
    t = proj.shape[0]
    ns = seq // tb
    w_q, w_v = GLA_QK, GLA_V

    def fwd(cb):
        return lambda b, i: (b * ns + i, cb)

    def bwd(cb):
        return lambda b, i: (b * ns + ns - 1 - i, cb)

    const = lambda b, i: (0, 0)
    in_specs = [
        pl.BlockSpec((tb, w_q), fwd(COL_Q // w_q)),
        pl.BlockSpec((tb, w_q), fwd(COL_K // w_q)),
        pl.BlockSpec((tb, w_v), fwd(COL_V // w_v)),
        pl.BlockSpec((tb, LR_WIDTH), fwd(COL_LR // LR_WIDTH)),
        pl.BlockSpec((tb, w_q), bwd(COL_Q // w_q)),
        pl.BlockSpec((tb, w_q), bwd(COL_K // w_q)),
        pl.BlockSpec((tb, w_v), bwd(COL_V // w_v)),
        pl.BlockSpec((tb, LR_WIDTH), bwd(COL_LR // LR_WIDTH)),
        pl.BlockSpec((LR_WIDTH, GLA_QK), const),
        pl.BlockSpec((LR_WIDTH, GLA_QK), const),
        pl.BlockSpec((1, GLA_QK), const),
        pl.BlockSpec((LR_WIDTH, GLA_QK), const),
        pl.BlockSpec((LR_WIDTH, GLA_QK), const),
        pl.BlockSpec((1, GLA_QK), const),
    ]
    out_specs = [
        pl.BlockSpec((tb, w_v), lambda b, i: (b * ns + i, 0)),
        pl.BlockSpec((tb, w_v), lambda b, i: (b * ns + ns - 1 - i, 0)),
    ]
    w2f_hi, w2f_lo = w2f
    w2b_hi, w2b_lo = w2b
    return pl.pallas_call(
        functools.partial(_gla_kernel, nchunk=tb // GLA_CHUNK),
        out_shape=[jax.ShapeDtypeStruct((t, w_v), F32), jax.ShapeDtypeStruct((t, w_v), F32)],
        grid=(nbatch, ns),
        in_specs=in_specs,
        out_specs=out_specs,
        scratch_shapes=[pltpu.VMEM((GLA_HEADS, GLA_DV, GLA_DK), F32),
                        pltpu.VMEM((GLA_HEADS, GLA_DV, GLA_DK), F32)],
        compiler_params=_cparams(("arbitrary", "arbitrary")),
        name="gla",
    )(proj, proj, proj, proj, proj, proj, proj, proj, w2f_hi, w2f_lo, bf, w2b_hi, w2b_lo, bb)


def _mix_kernel(x_ref, gate_ref, pool_ref, pprev_ref, pnext_ref, r_ref, qm_ref, of_ref, ob_ref,
                kv_ref, poolw_ref, pscale_ref, gnorm_ref, wup_pool_ref, wup_gla_ref, wup_mem_ref,
                wout_ref, gffn_ref, wr_hi_ref, wr_lo_ref,
                x1_ref, h2_ref, aff_ref, pp_ref, band_ref, *, tm, seq):
    s0 = (pl.program_id(0) % (seq // tm)) * tm
    hal = POOL_PAD
    win = MIX_SUB + 2 * hal

    @pl.when(pl.program_id(0) == 0)
    def _():
        t_i = lax.broadcasted_iota(I32, (MIX_SUB, win), 0)
        j_i = lax.broadcasted_iota(I32, (MIX_SUB, win), 1)
        for g, w in enumerate(POOL_WINDOWS):
            inside = (j_i - hal - t_i + w // 2).astype(jnp.uint32) < jnp.uint32(w)
            band_ref[g] = jnp.where(inside, 1.0, 0.0).astype(BF16)

    pp_ref[0:hal, :] = jnp.where(s0 > 0, pprev_ref[...], 0.0).astype(BF16)
    pp_ref[hal:hal + tm, :] = pool_ref[...].astype(BF16)
    pp_ref[hal + tm:hal + tm + hal, :] = jnp.where(s0 + tm < seq, pnext_ref[...], 0.0).astype(BF16)
    subs = [slice(r0, r0 + MIX_SUB) for r0 in range(0, tm, MIX_SUB)]
    kv = kv_ref[0]

    groups = [slice(g * POOL_GROUP_DIM, (g + 1) * POOL_GROUP_DIM) for g in range(len(POOL_WINDOWS))]
    mheads = [slice(h * MEM_HEAD_DIM, (h + 1) * MEM_HEAD_DIM) for h in range(MEM_HEADS)]

    tots = [[_dot(band_ref[g], pp_ref[rs.start:rs.start + win, cs]) for g, cs in enumerate(groups)] for rs in subs]
    qms = [qm_ref[rs, :].astype(BF16) for rs in subs]
    scores = [[_dot_nt(qm[:, hs], kv[:, hs]) * (MEM_HEAD_DIM ** -0.5) for hs in mheads] for qm in qms]

    def pool_maps(rs, tot):
        pos = s0 + rs.start + lax.broadcasted_iota(I32, (MIX_SUB, 1), 0)
        out = []
        for g, w in enumerate(POOL_WINDOWS):
            cnt = (jnp.minimum(pos + (w - w // 2), seq) - jnp.maximum(pos - w // 2, 0)).astype(F32)
            dlt = (tot[g] / cnt - pool_ref[rs, groups[g]]).astype(BF16)
            out.append(_dot(dlt, poolw_ref[g]))
        return (jnp.concatenate(out, axis=1) * pscale_ref[...]).astype(BF16)

    def mem_values(sc):
        om = []
        for h, s in enumerate(sc):
            e = jnp.exp(s - jnp.max(s, axis=-1, keepdims=True))
            p = (e / jnp.sum(e, axis=-1, keepdims=True)).astype(BF16)
            om.append(_dot(p, kv[:, MEM_DIM + h * MEM_HEAD_DIM:MEM_DIM + (h + 1) * MEM_HEAD_DIM]))
        return jnp.concatenate(om, axis=1).astype(BF16)

    def gla_out(rs):
        o = of_ref[rs, :] + ob_ref[rs, :]
        on = []
        for h in range(GLA_HEADS):
            oh = o[:, h * GLA_DV:(h + 1) * GLA_DV]
            on.append(oh * lax.rsqrt(jnp.mean(oh * oh, axis=-1, keepdims=True) + EPS))
        r = r_ref[rs, :]
        silu = r + jnp.tanh(r) * r
        return (jnp.concatenate(on, axis=1) * gnorm_ref[...] * silu).astype(BF16)

    y_pool = [pool_maps(rs, tot) for rs, tot in zip(subs, tots)]
    y_mem = [mem_values(sc) for sc in scores]
    y_gla = [gla_out(rs) for rs in subs]

    ups = [(_dot(yp, wup_pool_ref[...]), _dot(yg, wup_gla_ref[...]), _dot(ym, wup_mem_ref[...]))
           for yp, yg, ym in zip(y_pool, y_gla, y_mem)]

    def merge(rs, ups):
        gated = [y + jnp.tanh(gate_ref[rs, j * D_MODEL:(j + 1) * D_MODEL]) * y for j, y in enumerate(ups)]
        merged = (gated[0] + gated[1] + gated[2]).astype(BF16)
        x1 = x_ref[rs, :] + _dot(merged, wout_ref[...])
        x1_ref[rs, :] = x1
        return x1

    def route(rs, x1):
        h2 = _rms(x1, gffn_ref[...])
        h_hi, h_lo = _split2(h2)
        h2_ref[rs, :] = h_hi
        wr_hi, wr_lo = wr_hi_ref[...], wr_lo_ref[...]
        logits = _dot_nt(wr_hi, h_hi) + _dot_nt(wr_hi, h_lo) + _dot_nt(wr_lo, h_hi)
        e = jnp.exp(logits - jnp.max(logits, axis=0, keepdims=True))
        aff_ref[:, rs] = e / jnp.sum(e, axis=0, keepdims=True)

    x1s = [merge(rs, u) for rs, u in zip(subs, ups)]
    for rs, x1 in zip(subs, x1s):
        route(rs, x1)


def _mix(x2, proj, o_f, o_b, kv, wts, nbatch, seq, tm):
    t = x2.shape[0]
    ns = seq // tm
    h8 = tm // POOL_PAD
    n8 = t // POOL_PAD
    const2 = lambda i: (0, 0)
    in_specs = [
        pl.BlockSpec((tm, D_MODEL), lambda i: (i, 0)),
        pl.BlockSpec((tm, N_BRANCH * D_MODEL), lambda i: (i, COL_GATE // (N_BRANCH * D_MODEL))),
        pl.BlockSpec((tm, POOL_DIM), lambda i: (i, COL_POOL // POOL_DIM)),
        pl.BlockSpec((POOL_PAD, POOL_DIM), lambda i: (jnp.maximum(i * h8 - 1, 0), COL_POOL // POOL_DIM)),
        pl.BlockSpec((POOL_PAD, POOL_DIM), lambda i: (jnp.minimum((i + 1) * h8, n8 - 1), COL_POOL // POOL_DIM)),
        pl.BlockSpec((tm, GLA_V), lambda i: (i, COL_R // GLA_V)),
        pl.BlockSpec((tm, MEM_DIM), lambda i: (i, COL_QM // MEM_DIM)),
        pl.BlockSpec((tm, GLA_V), lambda i: (i, 0)),
        pl.BlockSpec((tm, GLA_V), lambda i: (i, 0)),
        pl.BlockSpec((1, N_MEM, 2 * MEM_DIM), lambda i: (i // ns, 0, 0)),
        pl.BlockSpec((len(POOL_WINDOWS), POOL_GROUP_DIM, POOL_GROUP_DIM), lambda i: (0, 0, 0)),
        pl.BlockSpec((1, POOL_DIM), const2),
        pl.BlockSpec((1, GLA_V), const2),
        pl.BlockSpec((POOL_DIM, D_MODEL), const2),
        pl.BlockSpec((GLA_V, D_MODEL), const2),
        pl.BlockSpec((MEM_DIM, D_MODEL), const2),
        pl.BlockSpec((D_MODEL, D_MODEL), const2),
        pl.BlockSpec((1, D_MODEL), const2),
        pl.BlockSpec((N_EXPERTS, D_MODEL), const2),
        pl.BlockSpec((N_EXPERTS, D_MODEL), const2),
    ]
    out_specs = [
        pl.BlockSpec((tm, D_MODEL), lambda i: (i, 0)),
        pl.BlockSpec((tm, D_MODEL), lambda i: (i, 0)),
        pl.BlockSpec((N_EXPERTS, tm), lambda i: (0, i)),
    ]
    return pl.pallas_call(
        functools.partial(_mix_kernel, tm=tm, seq=seq),
        out_shape=[jax.ShapeDtypeStruct((t, D_MODEL), F32), jax.ShapeDtypeStruct((t, D_MODEL), BF16),
                   jax.ShapeDtypeStruct((N_EXPERTS, t), F32)],
        grid=(t // tm,),
        in_specs=in_specs,
        out_specs=out_specs,
        scratch_shapes=[pltpu.VMEM((tm + 2 * POOL_PAD, POOL_DIM), BF16),
                        pltpu.VMEM((len(POOL_WINDOWS), MIX_SUB, MIX_SUB + 2 * POOL_PAD), BF16)],
        compiler_params=_cparams(("arbitrary",)),
        name="mix",
    )(x2, proj, proj, proj, proj, proj, proj, o_f, o_b, kv, *wts)


def _route_kernel(abt_ref, rank_ref, slo_ref, cnt_ref, thr_ref, *, nt, td, cap):
    ne = N_EXPERTS
    aff_all = abt_ref[...]

    def bit_step(i, thr):
        cand = thr | (jnp.int32(1) << (30 - i))
        hit = jnp.where(aff_all >= lax.bitcast_convert_type(cand, F32), 1.0, 0.0)
        cnt = jnp.sum(jnp.sum(hit, axis=1, keepdims=True), axis=2, keepdims=True)
        return jnp.where(cnt >= cap, cand, thr)

    thr = lax.fori_loop(0, 31, bit_step, jnp.zeros((ne, 1, 1), I32))
    thr_ref[...] = jnp.broadcast_to(lax.bitcast_convert_type(thr, F32), thr_ref.shape)

    t_row = lax.broadcasted_iota(I32, (td, td), 0)
    t_col = lax.broadcasted_iota(I32, (td, td), 1)
    upto = jnp.where(t_row <= t_col, 1.0, 0.0).astype(BF16)
    ones_t = jnp.ones((td, LANES), BF16)
    i_row = lax.broadcasted_iota(I32, (nt, nt), 0)
    i_col = lax.broadcasted_iota(I32, (nt, nt), 1)
    before = jnp.where(i_col < i_row, 1.0, 0.0).astype(BF16)

    def expert(e, carry):
        thr_e = thr_ref[e][0:1, :]
        aff = abt_ref[e]
        gt = jnp.where(aff > thr_e, 1.0, 0.0)
        eq = jnp.where(aff == thr_e, 1.0, 0.0)
        n_gt = jnp.sum(jnp.sum(gt, axis=0, keepdims=True), axis=1, keepdims=True)
        need = cap - n_gt
        eq16 = eq.astype(BF16)
        tot_eq = _dot(eq16, ones_t)
        ex_eq = _dot(before, tot_eq.astype(BF16))
        rel_eq = _dot(eq16, upto)
        sel = gt + eq * jnp.where(ex_eq[:, 0:1] + rel_eq <= need, 1.0, 0.0)
        rank_ref[e] = jnp.where(sel > 0.0, _dot(sel.astype(BF16), upto) - 1.0, -1.0)
        cnt = _dot(sel.astype(BF16), ones_t)
        pad = jnp.floor((cnt + float(ROW_ALIGN - 1)) * (1.0 / ROW_ALIGN)) * float(ROW_ALIGN)
        slo_ref[e] = _dot(before, pad.astype(BF16)).astype(I32)
        cnt_ref[e] = cnt.astype(I32)
        return carry

    lax.fori_loop(0, ne, expert, 0)


def _route(a_bt, cap):
    ne, nt, td = a_bt.shape
    return pl.pallas_call(
        functools.partial(_route_kernel, nt=nt, td=td, cap=cap),
        out_shape=[jax.ShapeDtypeStruct((ne, nt, td), F32), jax.ShapeDtypeStruct((ne, nt, LANES), I32),
                   jax.ShapeDtypeStruct((ne, nt, LANES), I32)],
        grid=(1,),
        in_specs=[pl.BlockSpec((ne, nt, td), lambda i: (0, 0, 0))],
        out_specs=[pl.BlockSpec((ne, nt, td), lambda i: (0, 0, 0)), pl.BlockSpec((ne, nt, LANES), lambda i: (0, 0, 0)),
                   pl.BlockSpec((ne, nt, LANES), lambda i: (0, 0, 0))],
        scratch_shapes=[pltpu.VMEM((ne, 8, td), F32)],
        compiler_params=_cparams(("arbitrary",)),
        name="route",
    )(a_bt)


def _segment_blocks(rank, chunk):
    td = rank.shape[1]
    r_iota = lax.broadcasted_iota(I32, (SEG_ROWS, td), 0).astype(F32) + jnp.asarray(chunk * SEG_ROWS, F32)
    return [jnp.where(rank[e:e + 1, :] == r_iota, 1.0, 0.0) for e in range(N_EXPERTS)]


def _segment_onehot(rank, chunk):
    return jnp.concatenate(_segment_blocks(rank, chunk), axis=0).astype(BF16)


def _dispatch_kernel(slo_ref, cnt_ref, vend_ref, rank_ref, h2_ref, aff_ref, xe_hbm, ge_hbm,
                     xbuf, gbuf, xov, gov, sem_x, sem_g, sem_ov, *, nt, cpad):
    i = pl.program_id(0)
    slot = i % 2

    @pl.when(i == 0)
    def _():
        xov[...] = jnp.zeros_like(xov)
        gov[...] = jnp.zeros_like(gov)

        def fill(e, first, rows, wait):
            row0 = pl.multiple_of(first, ROW_ALIGN)
            cx = pltpu.make_async_copy(xov.at[pl.ds(0, rows), :], xe_hbm.at[pl.ds(row0, rows), :], sem_ov)
            cg = pltpu.make_async_copy(gov.at[pl.ds(0, rows), :], ge_hbm.at[pl.ds(row0, rows), :], sem_ov)
            if wait:
                cx.wait()
                cg.wait()
            else:
                cx.start()
                cg.start()

        for wait in (False, True):
            for e in range(N_EXPERTS):
                tail0 = e * cpad + vend_ref[e]
                n_big = (cpad - vend_ref[e]) // SEG_ROWS
                n_small = ((cpad - vend_ref[e]) % SEG_ROWS) // ROW_ALIGN

                def big(k, c, tail0=tail0, e=e, wait=wait):
                    fill(e, tail0 + k * SEG_ROWS, SEG_ROWS, wait)
                    return c

                def small(k, c, tail0=tail0, n_big=n_big, e=e, wait=wait):
                    fill(e, tail0 + n_big * SEG_ROWS + k * ROW_ALIGN, ROW_ALIGN, wait)
                    return c

                lax.fori_loop(0, n_big, big, 0)
                lax.fori_loop(0, n_small, small, 0)

    def seg_copies(step, buf_slot, e):
        row0 = pl.multiple_of(slo_ref[step, e], ROW_ALIGN) + e * cpad
        src = pl.ds(e * SEG_ROWS, SEG_ROWS)
        return (pltpu.make_async_copy(xbuf.at[buf_slot, src, :], xe_hbm.at[pl.ds(row0, SEG_ROWS), :], sem_x),
                pltpu.make_async_copy(gbuf.at[buf_slot, src, :], ge_hbm.at[pl.ds(row0, SEG_ROWS), :], sem_g))

    def wait_step(step, buf_slot):
        for e in range(N_EXPERTS):
            cx, cg = seg_copies(step, buf_slot, e)
            cx.wait()
            cg.wait()

    rank = rank_ref[...]
    h2 = h2_ref[...]
    aff = aff_ref[...]

    def rows_for(chunk):
        blocks = _segment_blocks(rank, chunk)
        xr = _dot(jnp.concatenate(blocks, axis=0).astype(BF16), h2).astype(BF16)
        gs = [jnp.broadcast_to(jnp.sum(blocks[e] * aff[e:e + 1, :], axis=1, keepdims=True), (SEG_ROWS, LANES))
              for e in range(N_EXPERTS)]
        return xr, jnp.concatenate(gs, axis=0)

    xr, gr = rows_for(0)
    xbuf[slot] = xr
    gbuf[slot] = gr

    @pl.when(i > 0)
    def _():
        wait_step(i - 1, 1 - slot)

    for e in range(N_EXPERTS):
        cx, cg = seg_copies(i, slot, e)
        cx.start()
        cg.start()

    cmax = cnt_ref[i, 0]
    for e in range(1, N_EXPERTS):
        cmax = jnp.maximum(cmax, cnt_ref[i, e])

    def extra_chunk(chunk, carry):
        xo, go = rows_for(chunk)
        xov[...] = xo
        gov[...] = go
        for e in range(N_EXPERTS):
            @pl.when(cnt_ref[i, e] > chunk * SEG_ROWS)
            def _(e=e):
                row0 = pl.multiple_of(slo_ref[i, e] + chunk * SEG_ROWS, ROW_ALIGN) + e * cpad
                src = pl.ds(e * SEG_ROWS, SEG_ROWS)
                cx = pltpu.make_async_copy(xov.at[src, :], xe_hbm.at[pl.ds(row0, SEG_ROWS), :], sem_ov)
                cg = pltpu.make_async_copy(gov.at[src, :], ge_hbm.at[pl.ds(row0, SEG_ROWS), :], sem_ov)
                cx.start()
                cg.start()
                cx.wait()
                cg.wait()
        return carry

    lax.fori_loop(1, (cmax + SEG_ROWS - 1) // SEG_ROWS, extra_chunk, 0)

    @pl.when(i == nt - 1)
    def _():
        wait_step(i, slot)


def _dispatch(slo_t, cnt_t, vend, rank2, h2, aff, cpad):
    ne, t = rank2.shape
    td = DISPATCH_TILE
    nt = t // td
    grid_spec = pltpu.PrefetchScalarGridSpec(
        num_scalar_prefetch=3,
        grid=(nt,),
        in_specs=[
            pl.BlockSpec((ne, td), lambda i, s, c, v: (0, i)),
            pl.BlockSpec((td, D_MODEL), lambda i, s, c, v: (i, 0)),
            pl.BlockSpec((ne, td), lambda i, s, c, v: (0, i)),
        ],
        out_specs=[pl.BlockSpec(memory_space=pl.ANY), pl.BlockSpec(memory_space=pl.ANY)],
        scratch_shapes=[pltpu.VMEM((2, ne * SEG_ROWS, D_MODEL), BF16), pltpu.VMEM((2, ne * SEG_ROWS, LANES), F32),
                        pltpu.VMEM((ne * SEG_ROWS, D_MODEL), BF16), pltpu.VMEM((ne * SEG_ROWS, LANES), F32),
                        pltpu.SemaphoreType.DMA, pltpu.SemaphoreType.DMA, pltpu.SemaphoreType.DMA],
    )
    return pl.pallas_call(
        functools.partial(_dispatch_kernel, nt=nt, cpad=cpad),
        out_shape=[jax.ShapeDtypeStruct((ne * cpad, D_MODEL), BF16), jax.ShapeDtypeStruct((ne * cpad, LANES), F32)],
        grid_spec=grid_spec,
        compiler_params=_cparams(("arbitrary",)),
        name="dispatch",
    )(slo_t, cnt_t, vend, rank2, h2, aff)


def _ffn_kernel(vend_ref, x_ref, g_ref, wg_ref, wu_ref, wd_ref, o_ref, *, ft):
    e = pl.program_id(0)
    row0 = pl.program_id(1) * ft
    vend = vend_ref[e]

    @pl.when(row0 < vend)
    def _():
        valid = row0 + lax.broadcasted_iota(I32, (ft, 1), 0) < vend
        x = jnp.where(valid, x_ref[...], jnp.zeros((), BF16))
        hg = _dot(x, wg_ref[0])
        hu = _dot(x, wu_ref[0])
        hid = ((hg + jnp.tanh(hg) * hg) * hu).astype(BF16)
        gate = jnp.where(valid, g_ref[:, 0:1], 0.0)
        o_ref[...] = (_dot(hid, wd_ref[0]) * gate).astype(BF16)

    @pl.when(row0 >= vend)
    def _():
        o_ref[...] = jnp.zeros_like(o_ref)


def _ffn(vend, xe, ge, w_gate, w_up, w_down, cpad, ft):
    ne = N_EXPERTS
    nj = cpad // ft
    wspec = pl.BlockSpec((1, D_MODEL, D_MODEL), lambda e, j, v: (e, 0, 0))
    grid_spec = pltpu.PrefetchScalarGridSpec(
        num_scalar_prefetch=1,
        grid=(ne, nj),
        in_specs=[
            pl.BlockSpec((ft, D_MODEL), lambda e, j, v: (e * nj + j, 0)),
            pl.BlockSpec((ft, LANES), lambda e, j, v: (e * nj + j, 0)),
            wspec, wspec, wspec,
        ],
        out_specs=pl.BlockSpec((ft, D_MODEL), lambda e, j, v: (e * nj + j, 0)),
    )
    return pl.pallas_call(
        functools.partial(_ffn_kernel, ft=ft),
        out_shape=jax.ShapeDtypeStruct((ne * cpad, D_MODEL), BF16),
        grid_spec=grid_spec,
        compiler_params=_cparams(("arbitrary", "arbitrary")),
        name="ffn",
    )(vend, xe, ge, w_gate, w_up, w_down)


def _combine_kernel(slo_ref, cnt_ref, rank_ref, x1_ref, gfin_ref, ye_hbm, y_ref, ybuf, yov, acc_ref,
                    sem, sem_ov, *, nsteps, subs, cpad):
    i = pl.program_id(0)
    slot = i % 2
    td = DISPATCH_TILE

    def seg_copy(step, buf_slot, u, e):
        row0 = pl.multiple_of(slo_ref[step * subs + u, e], ROW_ALIGN) + e * cpad
        return pltpu.make_async_copy(ye_hbm.at[pl.ds(row0, SEG_ROWS), :],
                                     ybuf.at[buf_slot, u, pl.ds(e * SEG_ROWS, SEG_ROWS), :], sem.at[buf_slot])

    def fetch(step, buf_slot):
        for u in range(subs):
            for e in range(N_EXPERTS):
                seg_copy(step, buf_slot, u, e).start()

    @pl.when(i == 0)
    def _():
        fetch(0, 0)

    @pl.when(i + 1 < nsteps)
    def _():
        fetch(i + 1, 1 - slot)

    for u in range(subs):
        for e in range(N_EXPERTS):
            seg_copy(i, slot, u, e).wait()

    ranks = [rank_ref[:, u * td:(u + 1) * td] for u in range(subs)]
    onehots = [_segment_onehot(r, 0) for r in ranks]
    rows = [ybuf[slot, u] for u in range(subs)]
    for u in range(subs):
        acc_ref[u * td:(u + 1) * td, :] = x1_ref[u * td:(u + 1) * td, :] + _dot_tn(onehots[u], rows[u])

    for u in range(subs):
        tile = i * subs + u
        cmax = cnt_ref[tile, 0]
        for e in range(1, N_EXPERTS):
            cmax = jnp.maximum(cmax, cnt_ref[tile, e])

        def extra_chunk(chunk, carry, u=u, tile=tile):
            for e in range(N_EXPERTS):
                dst = pl.ds(e * SEG_ROWS, SEG_ROWS)

                @pl.when(cnt_ref[tile, e] > chunk * SEG_ROWS)
                def _(e=e, dst=dst):
                    row0 = pl.multiple_of(slo_ref[tile, e] + chunk * SEG_ROWS, ROW_ALIGN) + e * cpad
                    cp = pltpu.make_async_copy(ye_hbm.at[pl.ds(row0, SEG_ROWS), :], yov.at[dst, :], sem_ov)
                    cp.start()
                    cp.wait()

                @pl.when(cnt_ref[tile, e] <= chunk * SEG_ROWS)
                def _(dst=dst):
                    yov[dst, :] = jnp.zeros((SEG_ROWS, D_MODEL), BF16)
            acc_ref[u * td:(u + 1) * td, :] += _dot_tn(_segment_onehot(ranks[u], chunk), yov[...])
            return carry

        lax.fori_loop(1, (cmax + SEG_ROWS - 1) // SEG_ROWS, extra_chunk, 0)

    y_ref[...] = _rms(acc_ref[...], gfin_ref[...])


def _combine(slo_t, cnt_t, rank2, x1, g_final, ye, cpad):
    ne, t = rank2.shape
    nt = t // DISPATCH_TILE
    subs = 2 if nt % 2 == 0 else 1
    tc = subs * DISPATCH_TILE
    grid_spec = pltpu.PrefetchScalarGridSpec(
        num_scalar_prefetch=2,
        grid=(nt // subs,),
        in_specs=[
            pl.BlockSpec((ne, tc), lambda i, s, c: (0, i)),
            pl.BlockSpec((tc, D_MODEL), lambda i, s, c: (i, 0)),
            pl.BlockSpec((1, D_MODEL), lambda i, s, c: (0, 0)),
            pl.BlockSpec(memory_space=pl.ANY),
        ],
        out_specs=pl.BlockSpec((tc, D_MODEL), lambda i, s, c: (i, 0)),
        scratch_shapes=[pltpu.VMEM((2, subs, ne * SEG_ROWS, D_MODEL), BF16),
                        pltpu.VMEM((ne * SEG_ROWS, D_MODEL), BF16),
                        pltpu.VMEM((tc, D_MODEL), F32),
                        pltpu.SemaphoreType.DMA((2,)), pltpu.SemaphoreType.DMA],
    )
    return pl.pallas_call(
        functools.partial(_combine_kernel, nsteps=nt // subs, subs=subs, cpad=cpad),
        out_shape=jax.ShapeDtypeStruct((t, D_MODEL), F32),
        grid_spec=grid_spec,
        compiler_params=_cparams(("arbitrary",)),
        name="combine",
    )(slo_t, cnt_t, rank2, x1, g_final, ye)


def _prep_weights(norm_mix_g, w_in, gla_w2_f, gla_b_f, gla_w2_b, gla_b_b, gla_norm_g, pool_w, pool_scale,
                  mem_norm_g, w_mem_kv, w_up_pool, w_up_gla, w_up_mem, w_out, norm_ffn_g, w_router,
                  w_e_gate, w_e_up, w_e_down, norm_final_g):
    o_pool, o_q, o_k, o_v, o_r = 0, 512, 768, 1024, 1536
    o_lf, o_lb, o_qm, o_gate, o_end = 2048, 2064, 2080, 2592, 5664
    w = w_in[0]
    pad = jnp.zeros((D_MODEL, LR_WIDTH - 2 * GLA_GATE_RANK), F32)
    w_in_r = jnp.concatenate([w[:, o_gate:o_end] * 0.5, w[:, o_pool:o_q], w[:, o_v:o_r], w[:, o_r:o_lf] * 0.5,
                              w[:, o_qm:o_gate], w[:, o_q:o_k] * (GLA_DK ** -0.5), w[:, o_k:o_v], w[:, o_lf:o_lb],
                              w[:, o_lb:o_qm], pad], axis=1).astype(BF16)

    def pad_w2(w2, row0):
        full = jnp.zeros((LR_WIDTH, GLA_QK), F32).at[row0:row0 + GLA_GATE_RANK].set(w2)
        return _split2(full)

    row = lambda v: v.reshape(1, -1)
    wr_hi, wr_lo = _split2(w_router[0].T)
    return dict(
        norm_mix_g=row(norm_mix_g[0]), w_in_r=w_in_r,
        w2f=pad_w2(gla_w2_f[0], 0), bf=row(gla_b_f[0]),
        w2b=pad_w2(gla_w2_b[0], GLA_GATE_RANK), bb=row(gla_b_b[0]),
        mem_norm_g=row(mem_norm_g[0]), w_mem_kv=w_mem_kv[0].astype(BF16),
        mix=(pool_w[0].astype(BF16), row(pool_scale[0]), row(gla_norm_g[0]), w_up_pool[0].astype(BF16),
             w_up_gla[0].astype(BF16), w_up_mem[0].astype(BF16), (w_out[0] * 0.5).astype(BF16),
             row(norm_ffn_g[0]), wr_hi, wr_lo),
        w_e_gate=(w_e_gate[0] * 0.5).astype(BF16), w_e_up=w_e_up[0].astype(BF16), w_e_down=w_e_down[0].astype(BF16),
        norm_final_g=row(norm_final_g),
    )


def _pick(n, pref):
    for c in pref:
        if n % c == 0:
            return c
    raise ValueError(f"no tile for {n}")


def _trunk(x, mem, p):
    nbatch, seq, _ = x.shape
    t = nbatch * seq
    x2 = x.reshape(t, D_MODEL)
    tm = _pick(seq, (512, 256, 128))
    tb = _pick(seq, (512, 256, 128))
    kv = _mem_kv(mem, p["mem_norm_g"], p["w_mem_kv"])
    proj = _inproj(x2, p["norm_mix_g"], p["w_in_r"], tm)
    o_f, o_b = _gla(proj, p["w2f"], p["bf"], p["w2b"], p["bb"], nbatch, seq, tb)
    x1, h2, aff = _mix(x2, proj, o_f, o_b, kv, p["mix"], nbatch, seq, _pick(seq, (2 * MIX_SUB, MIX_SUB)))
    cap = max(1, min(t, EC_CAPACITY_FACTOR * t // N_EXPERTS))
    nt = t // DISPATCH_TILE
    rank, slo, cnt = _route(aff.reshape(N_EXPERTS, nt, DISPATCH_TILE), cap)
    rank2 = rank.reshape(N_EXPERTS, t)
    slo_t = slo[:, :, 0].T
    cnt_t = cnt[:, :, 0].T
    last_chunks = jnp.maximum(-(-cnt_t[nt - 1] // SEG_ROWS), 1)
    vend = slo_t[nt - 1] + last_chunks * SEG_ROWS
    cpad = -(-(cap + (ROW_ALIGN - 1) * nt + SEG_ROWS) // FFN_TILE) * FFN_TILE
    xe, ge = _dispatch(slo_t, cnt_t, vend, rank2, h2, aff, cpad)
    ye = _ffn(vend, xe, ge, p["w_e_gate"], p["w_e_up"], p["w_e_down"], cpad, FFN_TILE)
    y = _combine(slo_t, cnt_t, rank2, x1, p["norm_final_g"], ye, cpad)
    return y.reshape(nbatch, seq, D_MODEL)


def kernel(x_prompt, x_sample, mem_prompt, mem_sample, norm_mix_g, w_in, gla_w2_f, gla_b_f, gla_w2_b, gla_b_b, gla_norm_g, pool_w, pool_scale, mem_norm_g, w_mem_kv, w_up_pool, w_up_gla, w_up_mem, w_out, norm_ffn_g, w_router, w_e_gate, w_e_up, w_e_down, norm_final_g):
    p = _prep_weights(norm_mix_g, w_in, gla_w2_f, gla_b_f, gla_w2_b, gla_b_b, gla_norm_g, pool_w, pool_scale,
                      mem_norm_g, w_mem_kv, w_up_pool, w_up_gla, w_up_mem, w_out, norm_ffn_g, w_router,
                      w_e_gate, w_e_up, w_e_down, norm_final_g)
    return (_trunk(x_prompt, mem_prompt, p), _trunk(x_sample, mem_sample, p))
```

```python
import functools

import jax
import jax.numpy as jnp
from jax import lax
from jax.experimental import pallas as pl
from jax.experimental.pallas import tpu as pltpu

F32 = jnp.float32
BF16 = jnp.bfloat16
I32 = jnp.int32

D_MODEL = 1024
N_MEM = 256
POOL_WINDOWS = (2, 4, 8, 16)
POOL_GROUP_DIM = 128
POOL_DIM = 512
POOL_PAD = 64
GLA_HEADS = 4
GLA_DK = 64
GLA_DV = 128
GLA_QK = 256
GLA_V = 512
GLA_GATE_RANK = 16
GLA_GATE_TAU = 16.0
GLA_CHUNK = 64
MEM_HEADS = 4
MEM_HEAD_DIM = 128
MEM_DIM = 512
N_BRANCH = 3
N_EXPERTS = 16
EC_CAPACITY_FACTOR = 2
EPS = 1e-6

LANES = 128
MXU_DIM = 256

COL_GATE = 0
COL_POOL = COL_GATE + N_BRANCH * D_MODEL
COL_V = COL_POOL + POOL_DIM
COL_R = COL_V + GLA_V
COL_QM = COL_R + GLA_V
COL_Q = COL_QM + MEM_DIM
COL_K = COL_Q + GLA_QK
COL_LR = COL_K + GLA_QK
LR_WIDTH = MXU_DIM
PROJ_DIM = COL_LR + LR_WIDTH
PROJ_CHUNKS = ((0, 1536), (1536, 1536), (3072, 1536), (4608, 1280))

MIX_SUB = 128
DISPATCH_TILE = 256
SEG_ROWS = 48
ROW_ALIGN = 16
FFN_TILE = 512

VMEM_LIMIT = 56 * 1024 * 1024


def _cparams(semantics):
    return pltpu.CompilerParams(dimension_semantics=semantics, vmem_limit_bytes=VMEM_LIMIT)


def _rms(x, g):
    return x * lax.rsqrt(jnp.mean(x * x, axis=-1, keepdims=True) + EPS) * g


def _split2(x):
    hi = x.astype(BF16)
    lo = (x - hi.astype(F32)).astype(BF16)
    return hi, lo


def _split3(x):
    hi = x.astype(BF16)
    r = x - hi.astype(F32)
    mid = r.astype(BF16)
    lo = (r - mid.astype(F32)).astype(BF16)
    return hi, mid, lo


def _dot(a, b):
    return jnp.dot(a, b, preferred_element_type=F32)


def _dot_nt(a, b):
    return lax.dot_general(a, b, (((1,), (1,)), ((), ())), preferred_element_type=F32)


def _dot_tn(a, b):
    return lax.dot_general(a, b, (((0,), (0,)), ((), ())), preferred_element_type=F32)


def _dot3(a_hi, a_lo, b_hi, b_lo):
    return _dot(a_hi, b_hi) + _dot(a_lo, b_hi) + _dot(a_hi, b_lo)


def _memkv_kernel(mem_ref, g_ref, w_ref, o_ref):
    h = _rms(mem_ref[0], g_ref[...]).astype(BF16)
    o_ref[0] = _dot(h, w_ref[...]).astype(BF16)


def _mem_kv(mem, g, w_kv):
    nb = mem.shape[0]
    return pl.pallas_call(
        _memkv_kernel,
        out_shape=jax.ShapeDtypeStruct((nb, N_MEM, 2 * MEM_DIM), BF16),
        grid=(nb,),
        in_specs=[
            pl.BlockSpec((1, N_MEM, D_MODEL), lambda b: (b, 0, 0)),
            pl.BlockSpec((1, D_MODEL), lambda b: (0, 0)),
            pl.BlockSpec((D_MODEL, 2 * MEM_DIM), lambda b: (0, 0)),
        ],
        out_specs=pl.BlockSpec((1, N_MEM, 2 * MEM_DIM), lambda b: (b, 0, 0)),
        compiler_params=_cparams(("arbitrary",)),
        name="mem_kv",
    )(mem, g, w_kv)


def _inproj_kernel(x_ref, g_ref, w_ref, o_ref):
    h = _rms(x_ref[...], g_ref[...]).astype(BF16)
    for c0, cw in PROJ_CHUNKS:
        o_ref[:, c0:c0 + cw] = _dot(h, w_ref[:, c0:c0 + cw])


def _inproj(x2, g, w_in_r, tm):
    t = x2.shape[0]
    return pl.pallas_call(
        _inproj_kernel,
        out_shape=jax.ShapeDtypeStruct((t, PROJ_DIM), F32),
        grid=(t // tm,),
        in_specs=[
            pl.BlockSpec((tm, D_MODEL), lambda i: (i, 0)),
            pl.BlockSpec((1, D_MODEL), lambda i: (0, 0)),
            pl.BlockSpec((D_MODEL, PROJ_DIM), lambda i: (0, 0), pipeline_mode=pl.Buffered(1)),
        ],
        out_specs=pl.BlockSpec((tm, PROJ_DIM), lambda i: (i, 0)),
        compiler_params=_cparams(("arbitrary",)),
        name="inproj",
    )(x2, g, w_in_r)


def _gla_kernel(qf, kf, vf, lf, qb, kb, vb, lb, w2f_hi, w2f_lo, bf_ref, w2b_hi, w2b_lo, bb_ref,
                of_ref, ob_ref, sf_ref, sb_ref, *, nchunk):
    @pl.when(pl.program_id(1) == 0)
    def _():
        sf_ref[...] = jnp.zeros_like(sf_ref)
        sb_ref[...] = jnp.zeros_like(sb_ref)

    c = GLA_CHUNK
    pr = 2 * c
    dirs = (0, 1)
    heads = range(GLA_HEADS)
    ks = [slice(h * GLA_DK, (h + 1) * GLA_DK) for h in heads]
    vs = [slice(h * GLA_DV, (h + 1) * GLA_DV) for h in heads]
    pairs = [slice(p * pr, (p + 1) * pr) for p in range(nchunk // 2)]
    chunks = [slice(j * c, (j + 1) * c) for j in range(nchunk)]
    q = (qf[...], qb[...])
    k = (kf[...], kb[...])
    v16 = (vf[...].astype(BF16), vb[...].astype(BF16))
    st_refs = (sf_ref, sb_ref)

    lr = [_split2(l[...]) for l in (lf, lb)]
    w2 = ((w2f_hi[...], w2f_lo[...]), (w2b_hi[...], w2b_lo[...]))
    bias = (bf_ref[...], bb_ref[...])
    z = [_dot3(lr[d][0], lr[d][1], w2[d][0], w2[d][1]) + bias[d] for d in dirs]
    log_a = [(jnp.minimum(z[d], 0.0) - jnp.log(1.0 + jnp.exp(-jnp.abs(z[d])))) * (1.0 / GLA_GATE_TAU)
             for d in dirs]

    row = lax.broadcasted_iota(I32, (c, c), 0)
    col = lax.broadcasted_iota(I32, (c, c), 1)
    tri = (jnp.where(col <= row, 1.0, 0.0).astype(BF16), jnp.where(col >= row, 1.0, 0.0).astype(BF16))
    la = [_split2(log_a[d]) for d in dirs]
    bcum_c = [[_dot(tri[d], la[d][0][r]) + _dot(tri[d], la[d][1][r]) for r in chunks] for d in dirs]
    blast_c = [[bc[c - 1:c, :] if d == 0 else bc[0:1, :] for bc in bcum_c[d]] for d in dirs]
    bcum = [jnp.concatenate(bcum_c[d], axis=0) for d in dirs]
    blast = [jnp.concatenate([jnp.broadcast_to(b, (c, GLA_QK)) for b in blast_c[d]], axis=0) for d in dirs]
    q_s = [(q[d] * jnp.exp(bcum[d])).astype(BF16) for d in dirs]
    k_s = [(k[d] * jnp.exp(-bcum[d])).astype(BF16) for d in dirs]
    k_end = [(k[d] * jnp.exp(blast[d] - bcum[d])).astype(BF16) for d in dirs]

    prow = lax.broadcasted_iota(I32, (pr, pr), 0)
    pcol = lax.broadcasted_iota(I32, (pr, pr), 1)
    in_chunk = prow & (c - 1)
    keep = ((prow - pcol).astype(jnp.uint32) <= in_chunk.astype(jnp.uint32),
            (pcol - prow).astype(jnp.uint32) <= (c - 1 - in_chunk).astype(jnp.uint32))
    attn = [[[jnp.where(keep[d], _dot_nt(q_s[d][r, ks[h]], k_s[d][r, ks[h]]), 0.0).astype(BF16)
              for r in pairs] for h in heads] for d in dirs]
    kv = [[[_dot_tn(v16[d][r, vs[h]], k_end[d][r, ks[h]]) for r in chunks] for h in heads]
          for d in dirs]
    intra = [[jnp.concatenate([_dot(attn[d][h][p], v16[d][r, vs[h]]) for p, r in enumerate(pairs)], axis=0)
              for h in heads] for d in dirs]

    st = [[st_refs[d][h] for h in heads] for d in dirs]
    inter = [[[None] * nchunk for _ in heads] for _ in dirs]
    for step in range(nchunk):
        for d in dirs:
            j = step if d == 0 else nchunk - 1 - step
            dec = jnp.exp(blast_c[d][j])
            for h in heads:
                inter[d][h][j] = _dot_nt(q_s[d][chunks[j], ks[h]], st[d][h].astype(BF16))
                st[d][h] = st[d][h] * dec[:, ks[h]] + kv[d][h][j]
    for d, o_ref in zip(dirs, (of_ref, ob_ref)):
        for h in heads:
            st_refs[d][h] = st[d][h]
        o_ref[...] = jnp.concatenate([intra[d][h] + jnp.concatenate(inter[d][h], axis=0) for h in heads],
                                     axis=1)


def _gla(proj, w2f, bf, w2b, bb, nbatch, seq, tb):
    t = proj.shape[0]
    ns = seq // tb
    w_q, w_v = GLA_QK, GLA_V

    def fwd(cb):
        return lambda b, i: (b * ns + i, cb)

    def bwd(cb):
        return lambda b, i: (b * ns + ns - 1 - i, cb)

    const = lambda b, i: (0, 0)
    in_specs = [
        pl.BlockSpec((tb, w_q), fwd(COL_Q // w_q)),
        pl.BlockSpec((tb, w_q), fwd(COL_K // w_q)),
        pl.BlockSpec((tb, w_v), fwd(COL_V // w_v)),
        pl.BlockSpec((tb, LR_WIDTH), fwd(COL_LR // LR_WIDTH)),
        pl.BlockSpec((tb, w_q), bwd(COL_Q // w_q)),
        pl.BlockSpec((tb, w_q), bwd(COL_K // w_q)),
        pl.BlockSpec((tb, w_v), bwd(COL_V // w_v)),
        pl.BlockSpec((tb, LR_WIDTH), bwd(COL_LR // LR_WIDTH)),
        pl.BlockSpec((LR_WIDTH, GLA_QK), const),
        pl.BlockSpec((LR_WIDTH, GLA_QK), const),
        pl.BlockSpec((1, GLA_QK), const),
        pl.BlockSpec((LR_WIDTH, GLA_QK), const),
        pl.BlockSpec((LR_WIDTH, GLA_QK), const),
        pl.BlockSpec((1, GLA_QK), const),
    ]
    out_specs = [
        pl.BlockSpec((tb, w_v), lambda b, i: (b * ns + i, 0)),
        pl.BlockSpec((tb, w_v), lambda b, i: (b * ns + ns - 1 - i, 0)),
    ]
    w2f_hi, w2f_lo = w2f
    w2b_hi, w2b_lo = w2b
    return pl.pallas_call(
        functools.partial(_gla_kernel, nchunk=tb // GLA_CHUNK),
        out_shape=[jax.ShapeDtypeStruct((t, w_v), F32), jax.ShapeDtypeStruct((t, w_v), F32)],
        grid=(nbatch, ns),
        in_specs=in_specs,
        out_specs=out_specs,
        scratch_shapes=[pltpu.VMEM((GLA_HEADS, GLA_DV, GLA_DK), F32),
                        pltpu.VMEM((GLA_HEADS, GLA_DV, GLA_DK), F32)],
        compiler_params=_cparams(("arbitrary", "arbitrary")),
        name="gla",
    )(proj, proj, proj, proj, proj, proj, proj, proj, w2f_hi, w2f_lo, bf, w2b_hi, w2b_lo, bb)


def _mix_kernel(x_ref, gate_ref, pool_ref, pprev_ref, pnext_ref, r_ref, qm_ref, of_ref, ob_ref,
                kv_ref, poolw_ref, pscale_ref, gnorm_ref, wup_pool_ref, wup_gla_ref, wup_mem_ref,
                wout_ref, gffn_ref, wr_hi_ref, wr_lo_ref,
                x1_ref, h2_ref, aff_ref, pp_ref, band_ref, *, tm, seq):
    s0 = (pl.program_id(0) % (seq // tm)) * tm
    hal = POOL_PAD
    win = MIX_SUB + 2 * hal

    @pl.when(pl.program_id(0) == 0)
    def _():
        t_i = lax.broadcasted_iota(I32, (MIX_SUB, win), 0)
        j_i = lax.broadcasted_iota(I32, (MIX_SUB, win), 1)
        for g, w in enumerate(POOL_WINDOWS):
            inside = (j_i - hal - t_i + w // 2).astype(jnp.uint32) < jnp.uint32(w)
            band_ref[g] = jnp.where(inside, 1.0, 0.0).astype(BF16)

    pp_ref[0:hal, :] = jnp.where(s0 > 0, pprev_ref[...], 0.0).astype(BF16)
    pp_ref[hal:hal + tm, :] = pool_ref[...].astype(BF16)
    pp_ref[hal + tm:hal + tm + hal, :] = jnp.where(s0 + tm < seq, pnext_ref[...], 0.0).astype(BF16)
    subs = [slice(r0, r0 + MIX_SUB) for r0 in range(0, tm, MIX_SUB)]
    kv = kv_ref[0]

    groups = [slice(g * POOL_GROUP_DIM, (g + 1) * POOL_GROUP_DIM) for g in range(len(POOL_WINDOWS))]
    mheads = [slice(h * MEM_HEAD_DIM, (h + 1) * MEM_HEAD_DIM) for h in range(MEM_HEADS)]

    tots = [[_dot(band_ref[g], pp_ref[rs.start:rs.start + win, cs]) for g, cs in enumerate(groups)] for rs in subs]
    qms = [qm_ref[rs, :].astype(BF16) for rs in subs]
    scores = [[_dot_nt(qm[:, hs], kv[:, hs]) * (MEM_HEAD_DIM ** -0.5) for hs in mheads] for qm in qms]

    def pool_maps(rs, tot):
        pos = s0 + rs.start + lax.broadcasted_iota(I32, (MIX_SUB, 1), 0)
        out = []
        for g, w in enumerate(POOL_WINDOWS):
            cnt = (jnp.minimum(pos + (w - w // 2), seq) - jnp.maximum(pos - w // 2, 0)).astype(F32)
            dlt = (tot[g] / cnt - pool_ref[rs, groups[g]]).astype(BF16)
            out.append(_dot(dlt, poolw_ref[g]))
        return (jnp.concatenate(out, axis=1) * pscale_ref[...]).astype(BF16)

    def mem_values(sc):
        om = []
        for h, s in enumerate(sc):
            e = jnp.exp(s - jnp.max(s, axis=-1, keepdims=True))
            p = (e / jnp.sum(e, axis=-1, keepdims=True)).astype(BF16)
            om.append(_dot(p, kv[:, MEM_DIM + h * MEM_HEAD_DIM:MEM_DIM + (h + 1) * MEM_HEAD_DIM]))
        return jnp.concatenate(om, axis=1).astype(BF16)

    def gla_out(rs):
        o = of_ref[rs, :] + ob_ref[rs, :]
        on = []
        for h in range(GLA_HEADS):
            oh = o[:, h * GLA_DV:(h + 1) * GLA_DV]
            on.append(oh * lax.rsqrt(jnp.mean(oh * oh, axis=-1, keepdims=True) + EPS))
        r = r_ref[rs, :]
        silu = r + jnp.tanh(r) * r
        return (jnp.concatenate(on, axis=1) * gnorm_ref[...] * silu).astype(BF16)

    y_pool = [pool_maps(rs, tot) for rs, tot in zip(subs, tots)]
    y_mem = [mem_values(sc) for sc in scores]
    y_gla = [gla_out(rs) for rs in subs]

    ups = [(_dot(yp, wup_pool_ref[...]), _dot(yg, wup_gla_ref[...]), _dot(ym, wup_mem_ref[...]))
           for yp, yg, ym in zip(y_pool, y_gla, y_mem)]

    def merge(rs, ups):
        gated = [y + jnp.tanh(gate_ref[rs, j * D_MODEL:(j + 1) * D_MODEL]) * y for j, y in enumerate(ups)]
        merged = (gated[0] + gated[1] + gated[2]).astype(BF16)
        x1 = x_ref[rs, :] + _dot(merged, wout_ref[...])
        x1_ref[rs, :] = x1
        return x1

    def route(rs, x1):
        h2 = _rms(x1, gffn_ref[...])
        h_hi, h_lo = _split2(h2)
        h2_ref[rs, :] = h_hi
        wr_hi, wr_lo = wr_hi_ref[...], wr_lo_ref[...]
        logits = _dot_nt(wr_hi, h_hi) + _dot_nt(wr_hi, h_lo) + _dot_nt(wr_lo, h_hi)
        e = jnp.exp(logits - jnp.max(logits, axis=0, keepdims=True))
        aff_ref[:, rs] = e / jnp.sum(e, axis=0, keepdims=True)

    x1s = [merge(rs, u) for rs, u in zip(subs, ups)]
    for rs, x1 in zip(subs, x1s):
        route(rs, x1)


def _mix(x2, proj, o_f, o_b, kv, wts, nbatch, seq, tm):
    t = x2.shape[0]
    ns = seq // tm
    h8 = tm // POOL_PAD
    n8 = t // POOL_PAD
    const2 = lambda i: (0, 0)
    in_specs = [
        pl.BlockSpec((tm, D_MODEL), lambda i: (i, 0)),
        pl.BlockSpec((tm, N_BRANCH * D_MODEL), lambda i: (i, COL_GATE // (N_BRANCH * D_MODEL))),
        pl.BlockSpec((tm, POOL_DIM), lambda i: (i, COL_POOL // POOL_DIM)),
        pl.BlockSpec((POOL_PAD, POOL_DIM), lambda i: (jnp.maximum(i * h8 - 1, 0), COL_POOL // POOL_DIM)),
        pl.BlockSpec((POOL_PAD, POOL_DIM), lambda i: (jnp.minimum((i + 1) * h8, n8 - 1), COL_POOL // POOL_DIM)),
        pl.BlockSpec((tm, GLA_V), lambda i: (i, COL_R // GLA_V)),
        pl.BlockSpec((tm, MEM_DIM), lambda i: (i, COL_QM // MEM_DIM)),
        pl.BlockSpec((tm, GLA_V), lambda i: (i, 0)),
        pl.BlockSpec((tm, GLA_V), lambda i: (i, 0)),
        pl.BlockSpec((1, N_MEM, 2 * MEM_DIM), lambda i: (i // ns, 0, 0)),
        pl.BlockSpec((len(POOL_WINDOWS), POOL_GROUP_DIM, POOL_GROUP_DIM), lambda i: (0, 0, 0)),
        pl.BlockSpec((1, POOL_DIM), const2),
        pl.BlockSpec((1, GLA_V), const2),
        pl.BlockSpec((POOL_DIM, D_MODEL), const2),
        pl.BlockSpec((GLA_V, D_MODEL), const2),
        pl.BlockSpec((MEM_DIM, D_MODEL), const2),
        pl.BlockSpec((D_MODEL, D_MODEL), const2),
        pl.BlockSpec((1, D_MODEL), const2),
        pl.BlockSpec((N_EXPERTS, D_MODEL), const2),
        pl.BlockSpec((N_EXPERTS, D_MODEL), const2),
    ]
    out_specs = [
        pl.BlockSpec((tm, D_MODEL), lambda i: (i, 0)),
        pl.BlockSpec((tm, D_MODEL), lambda i: (i, 0)),
        pl.BlockSpec((N_EXPERTS, tm), lambda i: (0, i)),
    ]
    return pl.pallas_call(
        functools.partial(_mix_kernel, tm=tm, seq=seq),
        out_shape=[jax.ShapeDtypeStruct((t, D_MODEL), F32), jax.ShapeDtypeStruct((t, D_MODEL), BF16),
                   jax.ShapeDtypeStruct((N_EXPERTS, t), F32)],
        grid=(t // tm,),
        in_specs=in_specs,
        out_specs=out_specs,
        scratch_shapes=[pltpu.VMEM((tm + 2 * POOL_PAD, POOL_DIM), BF16),
                        pltpu.VMEM((len(POOL_WINDOWS), MIX_SUB, MIX_SUB + 2 * POOL_PAD), BF16)],
        compiler_params=_cparams(("arbitrary",)),
        name="mix",
    )(x2, proj, proj, proj, proj, proj, proj, o_f, o_b, kv, *wts)


def _route_kernel(abt_ref, rank_ref, slo_ref, cnt_ref, thr_ref, *, nt, td, cap):
    ne = N_EXPERTS
    aff_all = abt_ref[...]

    def bit_step(i, thr):
        cand = thr | (jnp.int32(1) << (30 - i))
        hit = jnp.where(aff_all >= lax.bitcast_convert_type(cand, F32), 1.0, 0.0)
        cnt = jnp.sum(jnp.sum(hit, axis=1, keepdims=True), axis=2, keepdims=True)
        return jnp.where(cnt >= cap, cand, thr)

    thr = lax.fori_loop(0, 31, bit_step, jnp.zeros((ne, 1, 1), I32))
    thr_ref[...] = jnp.broadcast_to(lax.bitcast_convert_type(thr, F32), thr_ref.shape)

    t_row = lax.broadcasted_iota(I32, (td, td), 0)
    t_col = lax.broadcasted_iota(I32, (td, td), 1)
    upto = jnp.where(t_row <= t_col, 1.0, 0.0).astype(BF16)
    ones_t = jnp.ones((td, LANES), BF16)
    i_row = lax.broadcasted_iota(I32, (nt, nt), 0)
    i_col = lax.broadcasted_iota(I32, (nt, nt), 1)
    before = jnp.where(i_col < i_row, 1.0, 0.0).astype(BF16)

    def expert(e, carry):
        thr_e = thr_ref[e][0:1, :]
        aff = abt_ref[e]
        gt = jnp.where(aff > thr_e, 1.0, 0.0)
        eq = jnp.where(aff == thr_e, 1.0, 0.0)
        n_gt = jnp.sum(jnp.sum(gt, axis=0, keepdims=True), axis=1, keepdims=True)
        need = cap - n_gt
        eq16 = eq.astype(BF16)
        tot_eq = _dot(eq16, ones_t)
        ex_eq = _dot(before, tot_eq.astype(BF16))
        rel_eq = _dot(eq16, upto)
        sel = gt + eq * jnp.where(ex_eq[:, 0:1] + rel_eq <= need, 1.0, 0.0)
        rank_ref[e] = jnp.where(sel > 0.0, _dot(sel.astype(BF16), upto) - 1.0, -1.0)
        cnt = _dot(sel.astype(BF16), ones_t)
        pad = jnp.floor((cnt + float(ROW_ALIGN - 1)) * (1.0 / ROW_ALIGN)) * float(ROW_ALIGN)
        slo_ref[e] = _dot(before, pad.astype(BF16)).astype(I32)
        cnt_ref[e] = cnt.astype(I32)
        return carry

    lax.fori_loop(0, ne, expert, 0)


def _route(a_bt, cap):
    ne, nt, td = a_bt.shape
    return pl.pallas_call(
        functools.partial(_route_kernel, nt=nt, td=td, cap=cap),
        out_shape=[jax.ShapeDtypeStruct((ne, nt, td), F32), jax.ShapeDtypeStruct((ne, nt, LANES), I32),
                   jax.ShapeDtypeStruct((ne, nt, LANES), I32)],
        grid=(1,),
        in_specs=[pl.BlockSpec((ne, nt, td), lambda i: (0, 0, 0))],
        out_specs=[pl.BlockSpec((ne, nt, td), lambda i: (0, 0, 0)), pl.BlockSpec((ne, nt, LANES), lambda i: (0, 0, 0)),
                   pl.BlockSpec((ne, nt, LANES), lambda i: (0, 0, 0))],
        scratch_shapes=[pltpu.VMEM((ne, 8, td), F32)],
        compiler_params=_cparams(("arbitrary",)),
        name="route",
    )(a_bt)


def _segment_blocks(rank, chunk):
    td = rank.shape[1]
    r_iota = lax.broadcasted_iota(I32, (SEG_ROWS, td), 0).astype(F32) + jnp.asarray(chunk * SEG_ROWS, F32)
    return [jnp.where(rank[e:e + 1, :] == r_iota, 1.0, 0.0) for e in range(N_EXPERTS)]


def _segment_onehot(rank, chunk):
    return jnp.concatenate(_segment_blocks(rank, chunk), axis=0).astype(BF16)


def _dispatch_kernel(slo_ref, cnt_ref, vend_ref, rank_ref, h2_ref, aff_ref, xe_hbm, ge_hbm,
                     xbuf, gbuf, xov, gov, sem_x, sem_g, sem_ov, *, nt, cpad):
    i = pl.program_id(0)
    slot = i % 2

    @pl.when(i == 0)
    def _():
        xov[...] = jnp.zeros_like(xov)
        gov[...] = jnp.zeros_like(gov)

        def fill(e, first, rows, wait):
            row0 = pl.multiple_of(first, ROW_ALIGN)
            cx = pltpu.make_async_copy(xov.at[pl.ds(0, rows), :], xe_hbm.at[pl.ds(row0, rows), :], sem_ov)
            cg = pltpu.make_async_copy(gov.at[pl.ds(0, rows), :], ge_hbm.at[pl.ds(row0, rows), :], sem_ov)
            if wait:
                cx.wait()
                cg.wait()
            else:
                cx.start()
                cg.start()

        for wait in (False, True):
            for e in range(N_EXPERTS):
                tail0 = e * cpad + vend_ref[e]
                n_big = (cpad - vend_ref[e]) // SEG_ROWS
                n_small = ((cpad - vend_ref[e]) % SEG_ROWS) // ROW_ALIGN

                def big(k, c, tail0=tail0, e=e, wait=wait):
                    fill(e, tail0 + k * SEG_ROWS, SEG_ROWS, wait)
                    return c

                def small(k, c, tail0=tail0, n_big=n_big, e=e, wait=wait):
                    fill(e, tail0 + n_big * SEG_ROWS + k * ROW_ALIGN, ROW_ALIGN, wait)
                    return c

                lax.fori_loop(0, n_big, big, 0)
                lax.fori_loop(0, n_small, small, 0)

    def seg_copies(step, buf_slot, e):
        row0 = pl.multiple_of(slo_ref[step, e], ROW_ALIGN) + e * cpad
        src = pl.ds(e * SEG_ROWS, SEG_ROWS)
        return (pltpu.make_async_copy(xbuf.at[buf_slot, src, :], xe_hbm.at[pl.ds(row0, SEG_ROWS), :], sem_x),
                pltpu.make_async_copy(gbuf.at[buf_slot, src, :], ge_hbm.at[pl.ds(row0, SEG_ROWS), :], sem_g))

    def wait_step(step, buf_slot):
        for e in range(N_EXPERTS):
            cx, cg = seg_copies(step, buf_slot, e)
            cx.wait()
            cg.wait()

    rank = rank_ref[...]
    h2 = h2_ref[...]
    aff = aff_ref[...]

    def rows_for(chunk):
        blocks = _segment_blocks(rank, chunk)
        xr = _dot(jnp.concatenate(blocks, axis=0).astype(BF16), h2).astype(BF16)
        gs = [jnp.broadcast_to(jnp.sum(blocks[e] * aff[e:e + 1, :], axis=1, keepdims=True), (SEG_ROWS, LANES))
              for e in range(N_EXPERTS)]
        return xr, jnp.concatenate(gs, axis=0)

    xr, gr = rows_for(0)
    xbuf[slot] = xr
    gbuf[slot] = gr

    @pl.when(i > 0)
    def _():
        wait_step(i - 1, 1 - slot)

    for e in range(N_EXPERTS):
        cx, cg = seg_copies(i, slot, e)
        cx.start()
        cg.start()

    cmax = cnt_ref[i, 0]
    for e in range(1, N_EXPERTS):
        cmax = jnp.maximum(cmax, cnt_ref[i, e])

    def extra_chunk(chunk, carry):
        xo, go = rows_for(chunk)
        xov[...] = xo
        gov[...] = go
        for e in range(N_EXPERTS):
            @pl.when(cnt_ref[i, e] > chunk * SEG_ROWS)
            def _(e=e):
                row0 = pl.multiple_of(slo_ref[i, e] + chunk * SEG_ROWS, ROW_ALIGN) + e * cpad
                src = pl.ds(e * SEG_ROWS, SEG_ROWS)
                cx = pltpu.make_async_copy(xov.at[src, :], xe_hbm.at[pl.ds(row0, SEG_ROWS), :], sem_ov)
                cg = pltpu.make_async_copy(gov.at[src, :], ge_hbm.at[pl.ds(row0, SEG_ROWS), :], sem_ov)
                cx.start()
                cg.start()
                cx.wait()
                cg.wait()
        return carry

    lax.fori_loop(1, (cmax + SEG_ROWS - 1) // SEG_ROWS, extra_chunk, 0)

    @pl.when(i == nt - 1)
    def _():
        wait_step(i, slot)


def _dispatch(slo_t, cnt_t, vend, rank2, h2, aff, cpad):
    ne, t = rank2.shape
    td = DISPATCH_TILE
    nt = t // td
    grid_spec = pltpu.PrefetchScalarGridSpec(
        num_scalar_prefetch=3,
        grid=(nt,),
        in_specs=[
            pl.BlockSpec((ne, td), lambda i, s, c, v: (0, i)),
            pl.BlockSpec((td, D_MODEL), lambda i, s, c, v: (i, 0)),
            pl.BlockSpec((ne, td), lambda i, s, c, v: (0, i)),
        ],
        out_specs=[pl.BlockSpec(memory_space=pl.ANY), pl.BlockSpec(memory_space=pl.ANY)],
        scratch_shapes=[pltpu.VMEM((2, ne * SEG_ROWS, D_MODEL), BF16), pltpu.VMEM((2, ne * SEG_ROWS, LANES), F32),
                        pltpu.VMEM((ne * SEG_ROWS, D_MODEL), BF16), pltpu.VMEM((ne * SEG_ROWS, LANES), F32),
                        pltpu.SemaphoreType.DMA, pltpu.SemaphoreType.DMA, pltpu.SemaphoreType.DMA],
    )
    return pl.pallas_call(
        functools.partial(_dispatch_kernel, nt=nt, cpad=cpad),
        out_shape=[jax.ShapeDtypeStruct((ne * cpad, D_MODEL), BF16), jax.ShapeDtypeStruct((ne * cpad, LANES), F32)],
        grid_spec=grid_spec,
        compiler_params=_cparams(("arbitrary",)),
        name="dispatch",
    )(slo_t, cnt_t, vend, rank2, h2, aff)


def _ffn_kernel(vend_ref, x_ref, g_ref, wg_ref, wu_ref, wd_ref, o_ref, *, ft):
    e = pl.program_id(0)
    row0 = pl.program_id(1) * ft
    vend = vend_ref[e]

    @pl.when(row0 < vend)
    def _():
        valid = row0 + lax.broadcasted_iota(I32, (ft, 1), 0) < vend
        x = jnp.where(valid, x_ref[...], jnp.zeros((), BF16))
        hg = _dot(x, wg_ref[0])
        hu = _dot(x, wu_ref[0])
        hid = ((hg + jnp.tanh(hg) * hg) * hu).astype(BF16)
        gate = jnp.where(valid, g_ref[:, 0:1], 0.0)
        o_ref[...] = (_dot(hid, wd_ref[0]) * gate).astype(BF16)

    @pl.when(row0 >= vend)
    def _():
        o_ref[...] = jnp.zeros_like(o_ref)


def _ffn(vend, xe, ge, w_gate, w_up, w_down, cpad, ft):
    ne = N_EXPERTS
    nj = cpad // ft
    wspec = pl.BlockSpec((1, D_MODEL, D_MODEL), lambda e, j, v: (e, 0, 0))
    grid_spec = pltpu.PrefetchScalarGridSpec(
        num_scalar_prefetch=1,
        grid=(ne, nj),
        in_specs=[
            pl.BlockSpec((ft, D_MODEL), lambda e, j, v: (e * nj + j, 0)),
            pl.BlockSpec((ft, LANES), lambda e, j, v: (e * nj + j, 0)),
            wspec, wspec, wspec,
        ],
        out_specs=pl.BlockSpec((ft, D_MODEL), lambda e, j, v: (e * nj + j, 0)),
    )
    return pl.pallas_call(
        functools.partial(_ffn_kernel, ft=ft),
        out_shape=jax.ShapeDtypeStruct((ne * cpad, D_MODEL), BF16),
        grid_spec=grid_spec,
        compiler_params=_cparams(("arbitrary", "arbitrary")),
        name="ffn",
    )(vend, xe, ge, w_gate, w_up, w_down)


def _combine_kernel(slo_ref, cnt_ref, rank_ref, x1_ref, gfin_ref, ye_hbm, y_ref, ybuf, yov, acc_ref,
                    sem, sem_ov, *, nsteps, subs, cpad):
    i = pl.program_id(0)
    slot = i % 2
    td = DISPATCH_TILE

    def seg_copy(step, buf_slot, u, e):
        row0 = pl.multiple_of(slo_ref[step * subs + u, e], ROW_ALIGN) + e * cpad
        return pltpu.make_async_copy(ye_hbm.at[pl.ds(row0, SEG_ROWS), :],
                                     ybuf.at[buf_slot, u, pl.ds(e * SEG_ROWS, SEG_ROWS), :], sem.at[buf_slot])

    def fetch(step, buf_slot):
        for u in range(subs):
            for e in range(N_EXPERTS):
                seg_copy(step, buf_slot, u, e).start()

    @pl.when(i == 0)
    def _():
        fetch(0, 0)

    @pl.when(i + 1 < nsteps)
    def _():
        fetch(i + 1, 1 - slot)

    for u in range(subs):
        for e in range(N_EXPERTS):
            seg_copy(i, slot, u, e).wait()

    ranks = [rank_ref[:, u * td:(u + 1) * td] for u in range(subs)]
    onehots = [_segment_onehot(r, 0) for r in ranks]
    rows = [ybuf[slot, u] for u in range(subs)]
    for u in range(subs):
        acc_ref[u * td:(u + 1) * td, :] = x1_ref[u * td:(u + 1) * td, :] + _dot_tn(onehots[u], rows[u])

    for u in range(subs):
        tile = i * subs + u
        cmax = cnt_ref[tile, 0]
        for e in range(1, N_EXPERTS):
            cmax = jnp.maximum(cmax, cnt_ref[tile, e])

        def extra_chunk(chunk, carry, u=u, tile=tile):
            for e in range(N_EXPERTS):
                dst = pl.ds(e * SEG_ROWS, SEG_ROWS)

                @pl.when(cnt_ref[tile, e] > chunk * SEG_ROWS)
                def _(e=e, dst=dst):
                    row0 = pl.multiple_of(slo_ref[tile, e] + chunk * SEG_ROWS, ROW_ALIGN) + e * cpad
                    cp = pltpu.make_async_copy(ye_hbm.at[pl.ds(row0, SEG_ROWS), :], yov.at[dst, :], sem_ov)
                    cp.start()
                    cp.wait()

                @pl.when(cnt_ref[tile, e] <= chunk * SEG_ROWS)
                def _(dst=dst):
                    yov[dst, :] = jnp.zeros((SEG_ROWS, D_MODEL), BF16)
            acc_ref[u * td:(u + 1) * td, :] += _dot_tn(_segment_onehot(ranks[u], chunk), yov[...])
            return carry

        lax.fori_loop(1, (cmax + SEG_ROWS - 1) // SEG_ROWS, extra_chunk, 0)

    y_ref[...] = _rms(acc_ref[...], gfin_ref[...])


def _combine(slo_t, cnt_t, rank2, x1, g_final, ye, cpad):
    ne, t = rank2.shape
    nt = t // DISPATCH_TILE
    subs = _pick(nt, (4, 2, 1))
    tc = subs * DISPATCH_TILE
    grid_spec = pltpu.PrefetchScalarGridSpec(
        num_scalar_prefetch=2,
        grid=(nt // subs,),
        in_specs=[
            pl.BlockSpec((ne, tc), lambda i, s, c: (0, i)),
            pl.BlockSpec((tc, D_MODEL), lambda i, s, c: (i, 0)),
            pl.BlockSpec((1, D_MODEL), lambda i, s, c: (0, 0)),
            pl.BlockSpec(memory_space=pl.ANY),
        ],
        out_specs=pl.BlockSpec((tc, D_MODEL), lambda i, s, c: (i, 0)),
        scratch_shapes=[pltpu.VMEM((2, subs, ne * SEG_ROWS, D_MODEL), BF16),
                        pltpu.VMEM((ne * SEG_ROWS, D_MODEL), BF16),
                        pltpu.VMEM((tc, D_MODEL), F32),
                        pltpu.SemaphoreType.DMA((2,)), pltpu.SemaphoreType.DMA],
    )
    return pl.pallas_call(
        functools.partial(_combine_kernel, nsteps=nt // subs, subs=subs, cpad=cpad),
        out_shape=jax.ShapeDtypeStruct((t, D_MODEL), F32),
        grid_spec=grid_spec,
        compiler_params=_cparams(("arbitrary",)),
        name="combine",
    )(slo_t, cnt_t, rank2, x1, g_final, ye)


def _prep_weights(norm_mix_g, w_in, gla_w2_f, gla_b_f, gla_w2_b, gla_b_b, gla_norm_g, pool_w, pool_scale,
                  mem_norm_g, w_mem_kv, w_up_pool, w_up_gla, w_up_mem, w_out, norm_ffn_g, w_router,
                  w_e_gate, w_e_up, w_e_down, norm_final_g):
    o_pool, o_q, o_k, o_v, o_r = 0, 512, 768, 1024, 1536
    o_lf, o_lb, o_qm, o_gate, o_end = 2048, 2064, 2080, 2592, 5664
    w = w_in[0]
    pad = jnp.zeros((D_MODEL, LR_WIDTH - 2 * GLA_GATE_RANK), F32)
    w_in_r = jnp.concatenate([w[:, o_gate:o_end] * 0.5, w[:, o_pool:o_q], w[:, o_v:o_r], w[:, o_r:o_lf] * 0.5,
                              w[:, o_qm:o_gate], w[:, o_q:o_k] * (GLA_DK ** -0.5), w[:, o_k:o_v], w[:, o_lf:o_lb],
                              w[:, o_lb:o_qm], pad], axis=1).astype(BF16)

    def pad_w2(w2, row0):
        full = jnp.zeros((LR_WIDTH, GLA_QK), F32).at[row0:row0 + GLA_GATE_RANK].set(w2)
        return _split2(full)

    row = lambda v: v.reshape(1, -1)
    wr_hi, wr_lo = _split2(w_router[0].T)
    return dict(
        norm_mix_g=row(norm_mix_g[0]), w_in_r=w_in_r,
        w2f=pad_w2(gla_w2_f[0], 0), bf=row(gla_b_f[0]),
        w2b=pad_w2(gla_w2_b[0], GLA_GATE_RANK), bb=row(gla_b_b[0]),
        mem_norm_g=row(mem_norm_g[0]), w_mem_kv=w_mem_kv[0].astype(BF16),
        mix=(pool_w[0].astype(BF16), row(pool_scale[0]), row(gla_norm_g[0]), w_up_pool[0].astype(BF16),
             w_up_gla[0].astype(BF16), w_up_mem[0].astype(BF16), (w_out[0] * 0.5).astype(BF16),
             row(norm_ffn_g[0]), wr_hi, wr_lo),
        w_e_gate=(w_e_gate[0] * 0.5).astype(BF16), w_e_up=w_e_up[0].astype(BF16), w_e_down=w_e_down[0].astype(BF16),
        norm_final_g=row(norm_final_g),
    )


def _pick(n, pref):
    for c in pref:
        if n % c == 0:
            return c
    raise ValueError(f"no tile for {n}")


def _trunk(x, mem, p):
    nbatch, seq, _ = x.shape
    t = nbatch * seq
    x2 = x.reshape(t, D_MODEL)
    tm = _pick(seq, (512, 256, 128))
    tb = _pick(seq, (512, 256, 128))
    kv = _mem_kv(mem, p["mem_norm_g"], p["w_mem_kv"])
    proj = _inproj(x2, p["norm_mix_g"], p["w_in_r"], tm)
    o_f, o_b = _gla(proj, p["w2f"], p["bf"], p["w2b"], p["bb"], nbatch, seq, tb)
    x1, h2, aff = _mix(x2, proj, o_f, o_b, kv, p["mix"], nbatch, seq, _pick(seq, (4 * MIX_SUB, MIX_SUB)))
    cap = max(1, min(t, EC_CAPACITY_FACTOR * t // N_EXPERTS))
    nt = t // DISPATCH_TILE
    rank, slo, cnt = _route(aff.reshape(N_EXPERTS, nt, DISPATCH_TILE), cap)
    rank2 = rank.reshape(N_EXPERTS, t)
    slo_t = slo[:, :, 0].T
    cnt_t = cnt[:, :, 0].T
    last_chunks = jnp.maximum(-(-cnt_t[nt - 1] // SEG_ROWS), 1)
    vend = slo_t[nt - 1] + last_chunks * SEG_ROWS
    cpad = -(-(cap + (ROW_ALIGN - 1) * nt + SEG_ROWS) // FFN_TILE) * FFN_TILE
    xe, ge = _dispatch(slo_t, cnt_t, vend, rank2, h2, aff, cpad)
    ye = _ffn(vend, xe, ge, p["w_e_gate"], p["w_e_up"], p["w_e_down"], cpad, FFN_TILE)
    y = _combine(slo_t, cnt_t, rank2, x1, p["norm_final_g"], ye, cpad)
    return y.reshape(nbatch, seq, D_MODEL)


def kernel(x_prompt, x_sample, mem_prompt, mem_sample, norm_mix_g, w_in, gla_w2_f, gla_b_f, gla_w2_b, gla_b_b, gla_norm_g, pool_w, pool_scale, mem_norm_g, w_mem_kv, w_up_pool, w_up_gla, w_up_mem, w_out, norm_ffn_g, w_router, w_e_gate, w_e_up, w_e_down, norm_final_g):
    p = _prep_weights(norm_mix_g, w_in, gla_w2_f, gla_b_f, gla_w2_b, gla_b_b, gla_norm_g, pool_w, pool_scale,
                      mem_norm_g, w_mem_kv, w_up_pool, w_up_gla, w_up_mem, w_out, norm_ffn_g, w_router,
                      w_e_gate, w_e_up, w_e_down, norm_final_g)
    return (_trunk(x_prompt, mem_prompt, p), _trunk(x_sample, mem_sample, p))
```

```python
import functools

import jax
import jax.numpy as jnp
from jax import lax
from jax.experimental import pallas as pl
from jax.experimental.pallas import tpu as pltpu

F32 = jnp.float32
BF16 = jnp.bfloat16
I32 = jnp.int32

D_MODEL = 1024
N_MEM = 256
POOL_WINDOWS = (2, 4, 8, 16)
POOL_GROUP_DIM = 128
POOL_DIM = 512
POOL_PAD = 64
GLA_HEADS = 4
GLA_DK = 64
GLA_DV = 128
GLA_QK = 256
GLA_V = 512
GLA_GATE_RANK = 16
GLA_GATE_TAU = 16.0
GLA_CHUNK = 64
MEM_HEADS = 4
MEM_HEAD_DIM = 128
MEM_DIM = 512
N_BRANCH = 3
N_EXPERTS = 16
EC_CAPACITY_FACTOR = 2
EPS = 1e-6

LANES = 128
MXU_DIM = 256

COL_GATE = 0
COL_POOL = COL_GATE + N_BRANCH * D_MODEL
COL_V = COL_POOL + POOL_DIM
COL_R = COL_V + GLA_V
COL_QM = COL_R + GLA_V
COL_Q = COL_QM + MEM_DIM
COL_K = COL_Q + GLA_QK
COL_LR = COL_K + GLA_QK
LR_WIDTH = MXU_DIM
PROJ_DIM = COL_LR + LR_WIDTH
PROJ_CHUNKS = ((0, 1536), (1536, 1536), (3072, 1536), (4608, 1280))

MIX_SUB = 128
DISPATCH_TILE = 256
SEG_ROWS = 48
ROW_ALIGN = 16
FFN_TILE = 512

VMEM_LIMIT = 56 * 1024 * 1024


def _cparams(semantics):
    return pltpu.CompilerParams(dimension_semantics=semantics, vmem_limit_bytes=VMEM_LIMIT)


def _rms(x, g):
    return x * lax.rsqrt(jnp.mean(x * x, axis=-1, keepdims=True) + EPS) * g


def _split2(x):
    hi = x.astype(BF16)
    lo = (x - hi.astype(F32)).astype(BF16)
    return hi, lo


def _split3(x):
    hi = x.astype(BF16)
    r = x - hi.astype(F32)
    mid = r.astype(BF16)
    lo = (r - mid.astype(F32)).astype(BF16)
    return hi, mid, lo


def _dot(a, b):
    return jnp.dot(a, b, preferred_element_type=F32)


def _dot_nt(a, b):
    return lax.dot_general(a, b, (((1,), (1,)), ((), ())), preferred_element_type=F32)


def _dot_tn(a, b):
    return lax.dot_general(a, b, (((0,), (0,)), ((), ())), preferred_element_type=F32)


def _dot3(a_hi, a_lo, b_hi, b_lo):
    return _dot(a_hi, b_hi) + _dot(a_lo, b_hi) + _dot(a_hi, b_lo)


def _memkv_kernel(mem_ref, g_ref, w_ref, o_ref):
    h = _rms(mem_ref[0], g_ref[...]).astype(BF16)
    o_ref[0] = _dot(h, w_ref[...]).astype(BF16)


def _mem_kv(mem, g, w_kv):
    nb = mem.shape[0]
    return pl.pallas_call(
        _memkv_kernel,
        out_shape=jax.ShapeDtypeStruct((nb, N_MEM, 2 * MEM_DIM), BF16),
        grid=(nb,),
        in_specs=[
            pl.BlockSpec((1, N_MEM, D_MODEL), lambda b: (b, 0, 0)),
            pl.BlockSpec((1, D_MODEL), lambda b: (0, 0)),
            pl.BlockSpec((D_MODEL, 2 * MEM_DIM), lambda b: (0, 0)),
        ],
        out_specs=pl.BlockSpec((1, N_MEM, 2 * MEM_DIM), lambda b: (b, 0, 0)),
        compiler_params=_cparams(("arbitrary",)),
        name="mem_kv",
    )(mem, g, w_kv)


def _inproj_kernel(x_ref, g_ref, w_ref, o_ref):
    h = _rms(x_ref[...], g_ref[...]).astype(BF16)
    for c0, cw in PROJ_CHUNKS:
        o_ref[:, c0:c0 + cw] = _dot(h, w_ref[:, c0:c0 + cw])


def _inproj(x2, g, w_in_r, tm):
    t = x2.shape[0]
    return pl.pallas_call(
        _inproj_kernel,
        out_shape=jax.ShapeDtypeStruct((t, PROJ_DIM), F32),
        grid=(t // tm,),
        in_specs=[
            pl.BlockSpec((tm, D_MODEL), lambda i: (i, 0)),
            pl.BlockSpec((1, D_MODEL), lambda i: (0, 0)),
            pl.BlockSpec((D_MODEL, PROJ_DIM), lambda i: (0, 0), pipeline_mode=pl.Buffered(1)),
        ],
        out_specs=pl.BlockSpec((tm, PROJ_DIM), lambda i: (i, 0)),
        compiler_params=_cparams(("arbitrary",)),
        name="inproj",
    )(x2, g, w_in_r)


def _gla_kernel(qf, kf, vf, lf, qb, kb, vb, lb, w2f_hi, w2f_lo, bf_ref, w2b_hi, w2b_lo, bb_ref,
                of_ref, ob_ref, sf_ref, sb_ref, *, nchunk):
    @pl.when(pl.program_id(1) == 0)
    def _():
        sf_ref[...] = jnp.zeros_like(sf_ref)
        sb_ref[...] = jnp.zeros_like(sb_ref)

    c = GLA_CHUNK
    pr = 2 * c
    dirs = (0, 1)
    heads = range(GLA_HEADS)
    ks = [slice(h * GLA_DK, (h + 1) * GLA_DK) for h in heads]
    vs = [slice(h * GLA_DV, (h + 1) * GLA_DV) for h in heads]
    pairs = [slice(p * pr, (p + 1) * pr) for p in range(nchunk // 2)]
    chunks = [slice(j * c, (j + 1) * c) for j in range(nchunk)]
    q = (qf[...], qb[...])
    k = (kf[...], kb[...])
    v16 = (vf[...].astype(BF16), vb[...].astype(BF16))
    st_refs = (sf_ref, sb_ref)

    lr = [_split2(l[...]) for l in (lf, lb)]
    w2 = ((w2f_hi[...], w2f_lo[...]), (w2b_hi[...], w2b_lo[...]))
    bias = (bf_ref[...], bb_ref[...])
    z = [_dot3(lr[d][0], lr[d][1], w2[d][0], w2[d][1]) + bias[d] for d in dirs]
    log_a = [(jnp.minimum(z[d], 0.0) - jnp.log(1.0 + jnp.exp(-jnp.abs(z[d])))) * (1.0 / GLA_GATE_TAU)
             for d in dirs]

    row = lax.broadcasted_iota(I32, (c, c), 0)
    col = lax.broadcasted_iota(I32, (c, c), 1)
    tri = (jnp.where(col <= row, 1.0, 0.0).astype(BF16), jnp.where(col >= row, 1.0, 0.0).astype(BF16))
    la = [_split2(log_a[d]) for d in dirs]
    bcum_c = [[_dot(tri[d], la[d][0][r]) + _dot(tri[d], la[d][1][r]) for r in chunks] for d in dirs]
    blast_c = [[bc[c - 1:c, :] if d == 0 else bc[0:1, :] for bc in bcum_c[d]] for d in dirs]
    bcum = [jnp.concatenate(bcum_c[d], axis=0) for d in dirs]
    blast = [jnp.concatenate([jnp.broadcast_to(b, (c, GLA_QK)) for b in blast_c[d]], axis=0) for d in dirs]
    q_s = [(q[d] * jnp.exp(bcum[d])).astype(BF16) for d in dirs]
    k_s = [(k[d] * jnp.exp(-bcum[d])).astype(BF16) for d in dirs]
    k_end = [(k[d] * jnp.exp(blast[d] - bcum[d])).astype(BF16) for d in dirs]

    prow = lax.broadcasted_iota(I32, (pr, pr), 0)
    pcol = lax.broadcasted_iota(I32, (pr, pr), 1)
    in_chunk = prow & (c - 1)
    keep = ((prow - pcol).astype(jnp.uint32) <= in_chunk.astype(jnp.uint32),
            (pcol - prow).astype(jnp.uint32) <= (c - 1 - in_chunk).astype(jnp.uint32))
    attn = [[[jnp.where(keep[d], _dot_nt(q_s[d][r, ks[h]], k_s[d][r, ks[h]]), 0.0).astype(BF16)
              for r in pairs] for h in heads] for d in dirs]
    kv = [[[_dot_tn(v16[d][r, vs[h]], k_end[d][r, ks[h]]) for r in chunks] for h in heads]
          for d in dirs]
    intra = [[jnp.concatenate([_dot(attn[d][h][p], v16[d][r, vs[h]]) for p, r in enumerate(pairs)], axis=0)
              for h in heads] for d in dirs]

    st = [[st_refs[d][h] for h in heads] for d in dirs]
    inter = [[[None] * nchunk for _ in heads] for _ in dirs]
    for step in range(nchunk):
        for d in dirs:
            j = step if d == 0 else nchunk - 1 - step
            dec = jnp.exp(blast_c[d][j])
            for h in heads:
                inter[d][h][j] = _dot_nt(q_s[d][chunks[j], ks[h]], st[d][h].astype(BF16))
                st[d][h] = st[d][h] * dec[:, ks[h]] + kv[d][h][j]
    for d, o_ref in zip(dirs, (of_ref, ob_ref)):
        for h in heads:
            st_refs[d][h] = st[d][h]
        o_ref[...] = jnp.concatenate([intra[d][h] + jnp.concatenate(inter[d][h], axis=0) for h in heads],
                                     axis=1)


def _gla(proj, w2f, bf, w2b, bb, nbatch, seq, tb):
    t = proj.shape[0]
    ns = seq // tb
    w_q, w_v = GLA_QK, GLA_V

    def fwd(cb):
        return lambda b, i: (b * ns + i, cb)

    def bwd(cb):
        return lambda b, i: (b * ns + ns - 1 - i, cb)

    const = lambda b, i: (0, 0)
    in_specs = [
        pl.BlockSpec((tb, w_q), fwd(COL_Q // w_q)),
        pl.BlockSpec((tb, w_q), fwd(COL_K // w_q)),
        pl.BlockSpec((tb, w_v), fwd(COL_V // w_v)),
        pl.BlockSpec((tb, LR_WIDTH), fwd(COL_LR // LR_WIDTH)),
        pl.BlockSpec((tb, w_q), bwd(COL_Q // w_q)),
        pl.BlockSpec((tb, w_q), bwd(COL_K // w_q)),
        pl.BlockSpec((tb, w_v), bwd(COL_V // w_v)),
        pl.BlockSpec((tb, LR_WIDTH), bwd(COL_LR // LR_WIDTH)),
        pl.BlockSpec((LR_WIDTH, GLA_QK), const),
        pl.BlockSpec((LR_WIDTH, GLA_QK), const),
        pl.BlockSpec((1, GLA_QK), const),
        pl.BlockSpec((LR_WIDTH, GLA_QK), const),
        pl.BlockSpec((LR_WIDTH, GLA_QK), const),
        pl.BlockSpec((1, GLA_QK), const),
    ]
    out_specs = [
        pl.BlockSpec((tb, w_v), lambda b, i: (b * ns + i, 0)),
        pl.BlockSpec((tb, w_v), lambda b, i: (b * ns + ns - 1 - i, 0)),
    ]
    w2f_hi, w2f_lo = w2f
    w2b_hi, w2b_lo = w2b
    return pl.pallas_call(
        functools.partial(_gla_kernel, nchunk=tb // GLA_CHUNK),
        out_shape=[jax.ShapeDtypeStruct((t, w_v), F32), jax.ShapeDtypeStruct((t, w_v), F32)],
        grid=(nbatch, ns),
        in_specs=in_specs,
        out_specs=out_specs,
        scratch_shapes=[pltpu.VMEM((GLA_HEADS, GLA_DV, GLA_DK), F32),
                        pltpu.VMEM((GLA_HEADS, GLA_DV, GLA_DK), F32)],
        compiler_params=_cparams(("arbitrary", "arbitrary")),
        name="gla",
    )(proj, proj, proj, proj, proj, proj, proj, proj, w2f_hi, w2f_lo, bf, w2b_hi, w2b_lo, bb)


def _mix_kernel(x_ref, gate_ref, pool_ref, pprev_ref, pnext_ref, r_ref, qm_ref, of_ref, ob_ref,
                kv_ref, poolw_ref, pscale_ref, gnorm_ref, wup_pool_ref, wup_gla_ref, wup_mem_ref,
                wout_ref, gffn_ref, wr_hi_ref, wr_lo_ref,
                x1_ref, h2_ref, aff_ref, pp_ref, band_ref, *, tm, seq):
    s0 = (pl.program_id(0) % (seq // tm)) * tm
    hal = POOL_PAD
    win = MIX_SUB + 2 * hal

    @pl.when(pl.program_id(0) == 0)
    def _():
        t_i = lax.broadcasted_iota(I32, (MIX_SUB, win), 0)
        j_i = lax.broadcasted_iota(I32, (MIX_SUB, win), 1)
        for g, w in enumerate(POOL_WINDOWS):
            inside = (j_i - hal - t_i + w // 2).astype(jnp.uint32) < jnp.uint32(w)
            band_ref[g] = jnp.where(inside, 1.0, 0.0).astype(BF16)

    pp_ref[0:hal, :] = jnp.where(s0 > 0, pprev_ref[...], 0.0).astype(BF16)
    pp_ref[hal:hal + tm, :] = pool_ref[...].astype(BF16)
    pp_ref[hal + tm:hal + tm + hal, :] = jnp.where(s0 + tm < seq, pnext_ref[...], 0.0).astype(BF16)
    subs = [slice(r0, r0 + MIX_SUB) for r0 in range(0, tm, MIX_SUB)]
    kv = kv_ref[0]

    groups = [slice(g * POOL_GROUP_DIM, (g + 1) * POOL_GROUP_DIM) for g in range(len(POOL_WINDOWS))]
    mheads = [slice(h * MEM_HEAD_DIM, (h + 1) * MEM_HEAD_DIM) for h in range(MEM_HEADS)]

    tots = [[_dot(band_ref[g], pp_ref[rs.start:rs.start + win, cs]) for g, cs in enumerate(groups)] for rs in subs]
    qms = [qm_ref[rs, :].astype(BF16) for rs in subs]
    scores = [[_dot_nt(qm[:, hs], kv[:, hs]) * (MEM_HEAD_DIM ** -0.5) for hs in mheads] for qm in qms]

    def pool_maps(rs, tot):
        pos = s0 + rs.start + lax.broadcasted_iota(I32, (MIX_SUB, 1), 0)
        out = []
        for g, w in enumerate(POOL_WINDOWS):
            cnt = (jnp.minimum(pos + (w - w // 2), seq) - jnp.maximum(pos - w // 2, 0)).astype(F32)
            dlt = (tot[g] / cnt - pool_ref[rs, groups[g]]).astype(BF16)
            out.append(_dot(dlt, poolw_ref[g]))
        return (jnp.concatenate(out, axis=1) * pscale_ref[...]).astype(BF16)

    def mem_values(sc):
        om = []
        for h, s in enumerate(sc):
            e = jnp.exp(s - jnp.max(s, axis=-1, keepdims=True))
            p = (e / jnp.sum(e, axis=-1, keepdims=True)).astype(BF16)
            om.append(_dot(p, kv[:, MEM_DIM + h * MEM_HEAD_DIM:MEM_DIM + (h + 1) * MEM_HEAD_DIM]))
        return jnp.concatenate(om, axis=1).astype(BF16)

    def gla_out(rs):
        o = of_ref[rs, :] + ob_ref[rs, :]
        on = []
        for h in range(GLA_HEADS):
            oh = o[:, h * GLA_DV:(h + 1) * GLA_DV]
            on.append(oh * lax.rsqrt(jnp.mean(oh * oh, axis=-1, keepdims=True) + EPS))
        r = r_ref[rs, :]
        silu = r + jnp.tanh(r) * r
        return (jnp.concatenate(on, axis=1) * gnorm_ref[...] * silu).astype(BF16)

    y_pool = [pool_maps(rs, tot) for rs, tot in zip(subs, tots)]
    y_mem = [mem_values(sc) for sc in scores]
    y_gla = [gla_out(rs) for rs in subs]

    ups = [(_dot(yp, wup_pool_ref[...]), _dot(yg, wup_gla_ref[...]), _dot(ym, wup_mem_ref[...]))
           for yp, yg, ym in zip(y_pool, y_gla, y_mem)]

    def merge(rs, ups):
        gated = [y + jnp.tanh(gate_ref[rs, j * D_MODEL:(j + 1) * D_MODEL]) * y for j, y in enumerate(ups)]
        merged = (gated[0] + gated[1] + gated[2]).astype(BF16)
        x1 = x_ref[rs, :] + _dot(merged, wout_ref[...])
        x1_ref[rs, :] = x1
        return x1

    def route(rs, x1):
        h2 = _rms(x1, gffn_ref[...])
        h_hi, h_lo = _split2(h2)
        h2_ref[rs, :] = h_hi
        wr_hi, wr_lo = wr_hi_ref[...], wr_lo_ref[...]
        logits = _dot_nt(wr_hi, h_hi) + _dot_nt(wr_hi, h_lo) + _dot_nt(wr_lo, h_hi)
        e = jnp.exp(logits - jnp.max(logits, axis=0, keepdims=True))
        aff_ref[:, rs] = e / jnp.sum(e, axis=0, keepdims=True)

    x1s = [merge(rs, u) for rs, u in zip(subs, ups)]
    for rs, x1 in zip(subs, x1s):
        route(rs, x1)


def _mix(x2, proj, o_f, o_b, kv, wts, nbatch, seq, tm):
    t = x2.shape[0]
    ns = seq // tm
    h8 = tm // POOL_PAD
    n8 = t // POOL_PAD
    const2 = lambda i: (0, 0)
    in_specs = [
        pl.BlockSpec((tm, D_MODEL), lambda i: (i, 0)),
        pl.BlockSpec((tm, N_BRANCH * D_MODEL), lambda i: (i, COL_GATE // (N_BRANCH * D_MODEL))),
        pl.BlockSpec((tm, POOL_DIM), lambda i: (i, COL_POOL // POOL_DIM)),
        pl.BlockSpec((POOL_PAD, POOL_DIM), lambda i: (jnp.maximum(i * h8 - 1, 0), COL_POOL // POOL_DIM)),
        pl.BlockSpec((POOL_PAD, POOL_DIM), lambda i: (jnp.minimum((i + 1) * h8, n8 - 1), COL_POOL // POOL_DIM)),
        pl.BlockSpec((tm, GLA_V), lambda i: (i, COL_R // GLA_V)),
        pl.BlockSpec((tm, MEM_DIM), lambda i: (i, COL_QM // MEM_DIM)),
        pl.BlockSpec((tm, GLA_V), lambda i: (i, 0)),
        pl.BlockSpec((tm, GLA_V), lambda i: (i, 0)),
        pl.BlockSpec((1, N_MEM, 2 * MEM_DIM), lambda i: (i // ns, 0, 0)),
        pl.BlockSpec((len(POOL_WINDOWS), POOL_GROUP_DIM, POOL_GROUP_DIM), lambda i: (0, 0, 0)),
        pl.BlockSpec((1, POOL_DIM), const2),
        pl.BlockSpec((1, GLA_V), const2),
        pl.BlockSpec((POOL_DIM, D_MODEL), const2),
        pl.BlockSpec((GLA_V, D_MODEL), const2),
        pl.BlockSpec((MEM_DIM, D_MODEL), const2),
        pl.BlockSpec((D_MODEL, D_MODEL), const2),
        pl.BlockSpec((1, D_MODEL), const2),
        pl.BlockSpec((N_EXPERTS, D_MODEL), const2),
        pl.BlockSpec((N_EXPERTS, D_MODEL), const2),
    ]
    out_specs = [
        pl.BlockSpec((tm, D_MODEL), lambda i: (i, 0)),
        pl.BlockSpec((tm, D_MODEL), lambda i: (i, 0)),
        pl.BlockSpec((N_EXPERTS, tm), lambda i: (0, i)),
    ]
    return pl.pallas_call(
        functools.partial(_mix_kernel, tm=tm, seq=seq),
        out_shape=[jax.ShapeDtypeStruct((t, D_MODEL), F32), jax.ShapeDtypeStruct((t, D_MODEL), BF16),
                   jax.ShapeDtypeStruct((N_EXPERTS, t), F32)],
        grid=(t // tm,),
        in_specs=in_specs,
        out_specs=out_specs,
        scratch_shapes=[pltpu.VMEM((tm + 2 * POOL_PAD, POOL_DIM), BF16),
                        pltpu.VMEM((len(POOL_WINDOWS), MIX_SUB, MIX_SUB + 2 * POOL_PAD), BF16)],
        compiler_params=_cparams(("arbitrary",)),
        name="mix",
    )(x2, proj, proj, proj, proj, proj, proj, o_f, o_b, kv, *wts)


def _route_kernel(abt_ref, rank_ref, slo_ref, cnt_ref, thr_ref, *, nt, td, cap):
    ne = N_EXPERTS
    aff_all = abt_ref[...]

    def bit_step(i, thr):
        cand = thr | (jnp.int32(1) << (30 - i))
        hit = jnp.where(aff_all >= lax.bitcast_convert_type(cand, F32), 1.0, 0.0)
        cnt = jnp.sum(jnp.sum(hit, axis=1, keepdims=True), axis=2, keepdims=True)
        return jnp.where(cnt >= cap, cand, thr)

    thr = lax.fori_loop(0, 31, bit_step, jnp.zeros((ne, 1, 1), I32))
    thr_ref[...] = jnp.broadcast_to(lax.bitcast_convert_type(thr, F32), thr_ref.shape)

    t_row = lax.broadcasted_iota(I32, (td, td), 0)
    t_col = lax.broadcasted_iota(I32, (td, td), 1)
    upto = jnp.where(t_row <= t_col, 1.0, 0.0).astype(BF16)
    ones_t = jnp.ones((td, LANES), BF16)
    i_row = lax.broadcasted_iota(I32, (nt, nt), 0)
    i_col = lax.broadcasted_iota(I32, (nt, nt), 1)
    before = jnp.where(i_col < i_row, 1.0, 0.0).astype(BF16)

    def expert(e, carry):
        thr_e = thr_ref[e][0:1, :]
        aff = abt_ref[e]
        gt = jnp.where(aff > thr_e, 1.0, 0.0)
        eq = jnp.where(aff == thr_e, 1.0, 0.0)
        n_gt = jnp.sum(jnp.sum(gt, axis=0, keepdims=True), axis=1, keepdims=True)
        need = cap - n_gt
        eq16 = eq.astype(BF16)
        tot_eq = _dot(eq16, ones_t)
        ex_eq = _dot(before, tot_eq.astype(BF16))
        rel_eq = _dot(eq16, upto)
        sel = gt + eq * jnp.where(ex_eq[:, 0:1] + rel_eq <= need, 1.0, 0.0)
        rank_ref[e] = jnp.where(sel > 0.0, _dot(sel.astype(BF16), upto) - 1.0, -1.0)
        cnt = _dot(sel.astype(BF16), ones_t)
        pad = jnp.floor((cnt + float(ROW_ALIGN - 1)) * (1.0 / ROW_ALIGN)) * float(ROW_ALIGN)
        slo_ref[e] = _dot(before, pad.astype(BF16)).astype(I32)
        cnt_ref[e] = cnt.astype(I32)
        return carry

    lax.fori_loop(0, ne, expert, 0)


def _route(a_bt, cap):
    ne, nt, td = a_bt.shape
    return pl.pallas_call(
        functools.partial(_route_kernel, nt=nt, td=td, cap=cap),
        out_shape=[jax.ShapeDtypeStruct((ne, nt, td), F32), jax.ShapeDtypeStruct((ne, nt, LANES), I32),
                   jax.ShapeDtypeStruct((ne, nt, LANES), I32)],
        grid=(1,),
        in_specs=[pl.BlockSpec((ne, nt, td), lambda i: (0, 0, 0))],
        out_specs=[pl.BlockSpec((ne, nt, td), lambda i: (0, 0, 0)), pl.BlockSpec((ne, nt, LANES), lambda i: (0, 0, 0)),
                   pl.BlockSpec((ne, nt, LANES), lambda i: (0, 0, 0))],
        scratch_shapes=[pltpu.VMEM((ne, 8, td), F32)],
        compiler_params=_cparams(("arbitrary",)),
        name="route",
    )(a_bt)


def _segment_blocks(rank, chunk):
    td = rank.shape[1]
    r_iota = lax.broadcasted_iota(I32, (SEG_ROWS, td), 0).astype(F32) + jnp.asarray(chunk * SEG_ROWS, F32)
    return [jnp.where(rank[e:e + 1, :] == r_iota, 1.0, 0.0) for e in range(N_EXPERTS)]


def _segment_onehot(rank, chunk):
    return jnp.concatenate(_segment_blocks(rank, chunk), axis=0).astype(BF16)


def _dispatch_kernel(slo_ref, cnt_ref, vend_ref, rank_ref, h2_ref, aff_ref, xe_hbm, ge_hbm,
                     xbuf, gbuf, xov, gov, sem_x, sem_g, sem_ov, *, nt, cpad):
    i = pl.program_id(0)
    slot = i % 2

    @pl.when(i == 0)
    def _():
        xov[...] = jnp.zeros_like(xov)
        gov[...] = jnp.zeros_like(gov)

        def fill(e, first, rows, wait):
            row0 = pl.multiple_of(first, ROW_ALIGN)
            cx = pltpu.make_async_copy(xov.at[pl.ds(0, rows), :], xe_hbm.at[pl.ds(row0, rows), :], sem_ov)
            cg = pltpu.make_async_copy(gov.at[pl.ds(0, rows), :], ge_hbm.at[pl.ds(row0, rows), :], sem_ov)
            if wait:
                cx.wait()
                cg.wait()
            else:
                cx.start()
                cg.start()

        for wait in (False, True):
            for e in range(N_EXPERTS):
                tail0 = e * cpad + vend_ref[e]
                n_big = (cpad - vend_ref[e]) // SEG_ROWS
                n_small = ((cpad - vend_ref[e]) % SEG_ROWS) // ROW_ALIGN

                def big(k, c, tail0=tail0, e=e, wait=wait):
                    fill(e, tail0 + k * SEG_ROWS, SEG_ROWS, wait)
                    return c

                def small(k, c, tail0=tail0, n_big=n_big, e=e, wait=wait):
                    fill(e, tail0 + n_big * SEG_ROWS + k * ROW_ALIGN, ROW_ALIGN, wait)
                    return c

                lax.fori_loop(0, n_big, big, 0)
                lax.fori_loop(0, n_small, small, 0)

    def seg_copies(step, buf_slot, e):
        row0 = pl.multiple_of(slo_ref[step, e], ROW_ALIGN) + e * cpad
        src = pl.ds(e * SEG_ROWS, SEG_ROWS)
        return (pltpu.make_async_copy(xbuf.at[buf_slot, src, :], xe_hbm.at[pl.ds(row0, SEG_ROWS), :], sem_x),
                pltpu.make_async_copy(gbuf.at[buf_slot, src, :], ge_hbm.at[pl.ds(row0, SEG_ROWS), :], sem_g))

    def wait_step(step, buf_slot):
        for e in range(N_EXPERTS):
            cx, cg = seg_copies(step, buf_slot, e)
            cx.wait()
            cg.wait()

    rank = rank_ref[...]
    h2 = h2_ref[...]
    aff = aff_ref[...]

    def rows_for(chunk):
        blocks = _segment_blocks(rank, chunk)
        xr = _dot(jnp.concatenate(blocks, axis=0).astype(BF16), h2).astype(BF16)
        gs = [jnp.broadcast_to(jnp.sum(blocks[e] * aff[e:e + 1, :], axis=1, keepdims=True), (SEG_ROWS, LANES))
              for e in range(N_EXPERTS)]
        return xr, jnp.concatenate(gs, axis=0)

    xr, gr = rows_for(0)
    xbuf[slot] = xr
    gbuf[slot] = gr

    @pl.when(i > 0)
    def _():
        wait_step(i - 1, 1 - slot)

    for e in range(N_EXPERTS):
        cx, cg = seg_copies(i, slot, e)
        cx.start()
        cg.start()

    cmax = cnt_ref[i, 0]
    for e in range(1, N_EXPERTS):
        cmax = jnp.maximum(cmax, cnt_ref[i, e])

    def extra_chunk(chunk, carry):
        xo, go = rows_for(chunk)
        xov[...] = xo
        gov[...] = go
        for e in range(N_EXPERTS):
            @pl.when(cnt_ref[i, e] > chunk * SEG_ROWS)
            def _(e=e):
                row0 = pl.multiple_of(slo_ref[i, e] + chunk * SEG_ROWS, ROW_ALIGN) + e * cpad
                src = pl.ds(e * SEG_ROWS, SEG_ROWS)
                cx = pltpu.make_async_copy(xov.at[src, :], xe_hbm.at[pl.ds(row0, SEG_ROWS), :], sem_ov)
                cg = pltpu.make_async_copy(gov.at[src, :], ge_hbm.at[pl.ds(row0, SEG_ROWS), :], sem_ov)
                cx.start()
                cg.start()
                cx.wait()
                cg.wait()
        return carry

    lax.fori_loop(1, (cmax + SEG_ROWS - 1) // SEG_ROWS, extra_chunk, 0)

    @pl.when(i == nt - 1)
    def _():
        wait_step(i, slot)


def _dispatch(slo_t, cnt_t, vend, rank2, h2, aff, cpad):
    ne, t = rank2.shape
    td = DISPATCH_TILE
    nt = t // td
    grid_spec = pltpu.PrefetchScalarGridSpec(
        num_scalar_prefetch=3,
        grid=(nt,),
        in_specs=[
            pl.BlockSpec((ne, td), lambda i, s, c, v: (0, i)),
            pl.BlockSpec((td, D_MODEL), lambda i, s, c, v: (i, 0)),
            pl.BlockSpec((ne, td), lambda i, s, c, v: (0, i)),
        ],
        out_specs=[pl.BlockSpec(memory_space=pl.ANY), pl.BlockSpec(memory_space=pl.ANY)],
        scratch_shapes=[pltpu.VMEM((2, ne * SEG_ROWS, D_MODEL), BF16), pltpu.VMEM((2, ne * SEG_ROWS, LANES), F32),
                        pltpu.VMEM((ne * SEG_ROWS, D_MODEL), BF16), pltpu.VMEM((ne * SEG_ROWS, LANES), F32),
                        pltpu.SemaphoreType.DMA, pltpu.SemaphoreType.DMA, pltpu.SemaphoreType.DMA],
    )
    return pl.pallas_call(
        functools.partial(_dispatch_kernel, nt=nt, cpad=cpad),
        out_shape=[jax.ShapeDtypeStruct((ne * cpad, D_MODEL), BF16), jax.ShapeDtypeStruct((ne * cpad, LANES), F32)],
        grid_spec=grid_spec,
        compiler_params=_cparams(("arbitrary",)),
        name="dispatch",
    )(slo_t, cnt_t, vend, rank2, h2, aff)


def _ffn_kernel(vend_ref, x_ref, g_ref, wg_ref, wu_ref, wd_ref, o_ref, *, ft, nj):
    e = pl.program_id(0)
    row0 = (nj - 1 - pl.program_id(1)) * ft
    vend = vend_ref[e]

    @pl.when(row0 < vend)
    def _():
        valid = row0 + lax.broadcasted_iota(I32, (ft, 1), 0) < vend
        x = jnp.where(valid, x_ref[...], jnp.zeros((), BF16))
        hg = _dot(x, wg_ref[0])
        hu = _dot(x, wu_ref[0])
        hid = ((hg + jnp.tanh(hg) * hg) * hu).astype(BF16)
        gate = jnp.where(valid, g_ref[:, 0:1], 0.0)
        o_ref[...] = (_dot(hid, wd_ref[0]) * gate).astype(BF16)

    @pl.when(row0 >= vend)
    def _():
        o_ref[...] = jnp.zeros_like(o_ref)


def _ffn(vend, xe, ge, w_gate, w_up, w_down, cpad, ft):
    ne = N_EXPERTS
    nj = cpad // ft
    wspec = pl.BlockSpec((1, D_MODEL, D_MODEL), lambda e, j, v: (e, 0, 0))
    grid_spec = pltpu.PrefetchScalarGridSpec(
        num_scalar_prefetch=1,
        grid=(ne, nj),
        in_specs=[
            pl.BlockSpec((ft, D_MODEL), lambda e, j, v: (e * nj + nj - 1 - j, 0)),
            pl.BlockSpec((ft, LANES), lambda e, j, v: (e * nj + nj - 1 - j, 0)),
            wspec, wspec, wspec,
        ],
        out_specs=pl.BlockSpec((ft, D_MODEL), lambda e, j, v: (e * nj + nj - 1 - j, 0)),
    )
    return pl.pallas_call(
        functools.partial(_ffn_kernel, ft=ft, nj=nj),
        out_shape=jax.ShapeDtypeStruct((ne * cpad, D_MODEL), BF16),
        grid_spec=grid_spec,
        compiler_params=_cparams(("arbitrary", "arbitrary")),
        name="ffn",
    )(vend, xe, ge, w_gate, w_up, w_down)


def _combine_kernel(slo_ref, cnt_ref, rank_ref, x1_ref, gfin_ref, ye_hbm, y_ref, ybuf, yov, acc_ref,
                    sem, sem_ov, *, nsteps, subs, cpad):
    i = pl.program_id(0)
    slot = i % 2
    td = DISPATCH_TILE

    def seg_copy(step, buf_slot, u, e):
        row0 = pl.multiple_of(slo_ref[step * subs + u, e], ROW_ALIGN) + e * cpad
        return pltpu.make_async_copy(ye_hbm.at[pl.ds(row0, SEG_ROWS), :],
                                     ybuf.at[buf_slot, u, pl.ds(e * SEG_ROWS, SEG_ROWS), :], sem.at[buf_slot])

    def fetch(step, buf_slot):
        for u in range(subs):
            for e in range(N_EXPERTS):
                seg_copy(step, buf_slot, u, e).start()

    @pl.when(i == 0)
    def _():
        fetch(0, 0)

    @pl.when(i + 1 < nsteps)
    def _():
        fetch(i + 1, 1 - slot)

    for u in range(subs):
        for e in range(N_EXPERTS):
            seg_copy(i, slot, u, e).wait()

    ranks = [rank_ref[:, u * td:(u + 1) * td] for u in range(subs)]
    onehots = [_segment_onehot(r, 0) for r in ranks]
    rows = [ybuf[slot, u] for u in range(subs)]
    for u in range(subs):
        acc_ref[u * td:(u + 1) * td, :] = x1_ref[u * td:(u + 1) * td, :] + _dot_tn(onehots[u], rows[u])

    for u in range(subs):
        tile = i * subs + u
        cmax = cnt_ref[tile, 0]
        for e in range(1, N_EXPERTS):
            cmax = jnp.maximum(cmax, cnt_ref[tile, e])

        def extra_chunk(chunk, carry, u=u, tile=tile):
            for e in range(N_EXPERTS):
                dst = pl.ds(e * SEG_ROWS, SEG_ROWS)

                @pl.when(cnt_ref[tile, e] > chunk * SEG_ROWS)
                def _(e=e, dst=dst):
                    row0 = pl.multiple_of(slo_ref[tile, e] + chunk * SEG_ROWS, ROW_ALIGN) + e * cpad
                    cp = pltpu.make_async_copy(ye_hbm.at[pl.ds(row0, SEG_ROWS), :], yov.at[dst, :], sem_ov)
                    cp.start()
                    cp.wait()

                @pl.when(cnt_ref[tile, e] <= chunk * SEG_ROWS)
                def _(dst=dst):
                    yov[dst, :] = jnp.zeros((SEG_ROWS, D_MODEL), BF16)
            acc_ref[u * td:(u + 1) * td, :] += _dot_tn(_segment_onehot(ranks[u], chunk), yov[...])
            return carry

        lax.fori_loop(1, (cmax + SEG_ROWS - 1) // SEG_ROWS, extra_chunk, 0)

    y_ref[...] = _rms(acc_ref[...], gfin_ref[...])


def _combine(slo_t, cnt_t, rank2, x1, g_final, ye, cpad):
    ne, t = rank2.shape
    nt = t // DISPATCH_TILE
    subs = _pick(nt, (4, 2, 1))
    tc = subs * DISPATCH_TILE
    grid_spec = pltpu.PrefetchScalarGridSpec(
        num_scalar_prefetch=2,
        grid=(nt // subs,),
        in_specs=[
            pl.BlockSpec((ne, tc), lambda i, s, c: (0, i)),
            pl.BlockSpec((tc, D_MODEL), lambda i, s, c: (i, 0)),
            pl.BlockSpec((1, D_MODEL), lambda i, s, c: (0, 0)),
            pl.BlockSpec(memory_space=pl.ANY),
        ],
        out_specs=pl.BlockSpec((tc, D_MODEL), lambda i, s, c: (i, 0)),
        scratch_shapes=[pltpu.VMEM((2, subs, ne * SEG_ROWS, D_MODEL), BF16),
                        pltpu.VMEM((ne * SEG_ROWS, D_MODEL), BF16),
                        pltpu.VMEM((tc, D_MODEL), F32),
                        pltpu.SemaphoreType.DMA((2,)), pltpu.SemaphoreType.DMA],
    )
    return pl.pallas_call(
        functools.partial(_combine_kernel, nsteps=nt // subs, subs=subs, cpad=cpad),
        out_shape=jax.ShapeDtypeStruct((t, D_MODEL), F32),
        grid_spec=grid_spec,
        compiler_params=_cparams(("arbitrary",)),
        name="combine",
    )(slo_t, cnt_t, rank2, x1, g_final, ye)


def _prep_weights(norm_mix_g, w_in, gla_w2_f, gla_b_f, gla_w2_b, gla_b_b, gla_norm_g, pool_w, pool_scale,
                  mem_norm_g, w_mem_kv, w_up_pool, w_up_gla, w_up_mem, w_out, norm_ffn_g, w_router,
                  w_e_gate, w_e_up, w_e_down, norm_final_g):
    o_pool, o_q, o_k, o_v, o_r = 0, 512, 768, 1024, 1536
    o_lf, o_lb, o_qm, o_gate, o_end = 2048, 2064, 2080, 2592, 5664
    w = w_in[0]
    pad = jnp.zeros((D_MODEL, LR_WIDTH - 2 * GLA_GATE_RANK), F32)
    w_in_r = jnp.concatenate([w[:, o_gate:o_end] * 0.5, w[:, o_pool:o_q], w[:, o_v:o_r], w[:, o_r:o_lf] * 0.5,
                              w[:, o_qm:o_gate], w[:, o_q:o_k] * (GLA_DK ** -0.5), w[:, o_k:o_v], w[:, o_lf:o_lb],
                              w[:, o_lb:o_qm], pad], axis=1).astype(BF16)

    def pad_w2(w2, row0):
        full = jnp.zeros((LR_WIDTH, GLA_QK), F32).at[row0:row0 + GLA_GATE_RANK].set(w2)
        return _split2(full)

    row = lambda v: v.reshape(1, -1)
    wr_hi, wr_lo = _split2(w_router[0].T)
    return dict(
        norm_mix_g=row(norm_mix_g[0]), w_in_r=w_in_r,
        w2f=pad_w2(gla_w2_f[0], 0), bf=row(gla_b_f[0]),
        w2b=pad_w2(gla_w2_b[0], GLA_GATE_RANK), bb=row(gla_b_b[0]),
        mem_norm_g=row(mem_norm_g[0]), w_mem_kv=w_mem_kv[0].astype(BF16),
        mix=(pool_w[0].astype(BF16), row(pool_scale[0]), row(gla_norm_g[0]), w_up_pool[0].astype(BF16),
             w_up_gla[0].astype(BF16), w_up_mem[0].astype(BF16), (w_out[0] * 0.5).astype(BF16),
             row(norm_ffn_g[0]), wr_hi, wr_lo),
        w_e_gate=(w_e_gate[0] * 0.5).astype(BF16), w_e_up=w_e_up[0].astype(BF16), w_e_down=w_e_down[0].astype(BF16),
        norm_final_g=row(norm_final_g),
    )


def _pick(n, pref):
    for c in pref:
        if n % c == 0:
            return c
    raise ValueError(f"no tile for {n}")


def _trunk(x, mem, p):
    nbatch, seq, _ = x.shape
    t = nbatch * seq
    x2 = x.reshape(t, D_MODEL)
    tm = _pick(seq, (512, 256, 128))
    tb = _pick(seq, (512, 256, 128))
    kv = _mem_kv(mem, p["mem_norm_g"], p["w_mem_kv"])
    proj = _inproj(x2, p["norm_mix_g"], p["w_in_r"], tm)
    o_f, o_b = _gla(proj, p["w2f"], p["bf"], p["w2b"], p["bb"], nbatch, seq, tb)
    x1, h2, aff = _mix(x2, proj, o_f, o_b, kv, p["mix"], nbatch, seq, _pick(seq, (4 * MIX_SUB, MIX_SUB)))
    cap = max(1, min(t, EC_CAPACITY_FACTOR * t // N_EXPERTS))
    nt = t // DISPATCH_TILE
    rank, slo, cnt = _route(aff.reshape(N_EXPERTS, nt, DISPATCH_TILE), cap)
    rank2 = rank.reshape(N_EXPERTS, t)
    slo_t = slo[:, :, 0].T
    cnt_t = cnt[:, :, 0].T
    last_chunks = jnp.maximum(-(-cnt_t[nt - 1] // SEG_ROWS), 1)
    vend = slo_t[nt - 1] + last_chunks * SEG_ROWS
    cpad = -(-(cap + (ROW_ALIGN - 1) * nt + SEG_ROWS) // FFN_TILE) * FFN_TILE
    xe, ge = _dispatch(slo_t, cnt_t, vend, rank2, h2, aff, cpad)
    ye = _ffn(vend, xe, ge, p["w_e_gate"], p["w_e_up"], p["w_e_down"], cpad, FFN_TILE)
    y = _combine(slo_t, cnt_t, rank2, x1, p["norm_final_g"], ye, cpad)
    return y.reshape(nbatch, seq, D_MODEL)


def kernel(x_prompt, x_sample, mem_prompt, mem_sample, norm_mix_g, w_in, gla_w2_f, gla_b_f, gla_w2_b, gla_b_b, gla_norm_g, pool_w, pool_scale, mem_norm_g, w_mem_kv, w_up_pool, w_up_gla, w_up_mem, w_out, norm_ffn_g, w_router, w_e_gate, w_e_up, w_e_down, norm_final_g):
    p = _prep_weights(norm_mix_g, w_in, gla_w2_f, gla_b_f, gla_w2_b, gla_b_b, gla_norm_g, pool_w, pool_scale,
                      mem_norm_g, w_mem_kv, w_up_pool, w_up_gla, w_up_mem, w_out, norm_ffn_g, w_router,
                      w_e_gate, w_e_up, w_e_down, norm_final_g)
    return (_trunk(x_prompt, mem_prompt, p), _trunk(x_sample, mem_sample, p))
```

```python
import functools

import jax
import jax.numpy as jnp
from jax import lax
from jax.experimental import pallas as pl
from jax.experimental.pallas import tpu as pltpu

F32 = jnp.float32
BF16 = jnp.bfloat16
I32 = jnp.int32

D_MODEL = 1024
N_MEM = 256
POOL_WINDOWS = (2, 4, 8, 16)
POOL_GROUP_DIM = 128
POOL_DIM = 512
POOL_PAD = 64
GLA_HEADS = 4
GLA_DK = 64
GLA_DV = 128
GLA_QK = 256
GLA_V = 512
GLA_GATE_RANK = 16
GLA_GATE_TAU = 16.0
GLA_CHUNK = 64
MEM_HEADS = 4
MEM_HEAD_DIM = 128
MEM_DIM = 512
N_BRANCH = 3
N_EXPERTS = 16
EC_CAPACITY_FACTOR = 2
EPS = 1e-6

LANES = 128
MXU_DIM = 256

COL_GATE = 0
COL_POOL = COL_GATE + N_BRANCH * D_MODEL
COL_V = COL_POOL + POOL_DIM
COL_R = COL_V + GLA_V
COL_QM = COL_R + GLA_V
COL_Q = COL_QM + MEM_DIM
COL_K = COL_Q + GLA_QK
COL_LR = COL_K + GLA_QK
LR_WIDTH = MXU_DIM
PROJ_DIM = COL_LR + LR_WIDTH
PROJ_CHUNKS = ((0, 1536), (1536, 1536), (3072, 1536), (4608, 1280))

MIX_SUB = 128
DISPATCH_TILE = 256
SEG_ROWS = 48
ROW_ALIGN = 16
FFN_TILE = 512

VMEM_LIMIT = 56 * 1024 * 1024


def _cparams(semantics):
    return pltpu.CompilerParams(dimension_semantics=semantics, vmem_limit_bytes=VMEM_LIMIT)


def _rms(x, g):
    return x * lax.rsqrt(jnp.mean(x * x, axis=-1, keepdims=True) + EPS) * g


def _split2(x):
    hi = x.astype(BF16)
    lo = (x - hi.astype(F32)).astype(BF16)
    return hi, lo


def _dot(a, b):
    return jnp.dot(a, b, preferred_element_type=F32)


def _dot_nt(a, b):
    return lax.dot_general(a, b, (((1,), (1,)), ((), ())), preferred_element_type=F32)


def _dot_tn(a, b):
    return lax.dot_general(a, b, (((0,), (0,)), ((), ())), preferred_element_type=F32)


def _dot3(a_hi, a_lo, b_hi, b_lo):
    return _dot(a_hi, b_hi) + _dot(a_lo, b_hi) + _dot(a_hi, b_lo)


def _memkv_kernel(mem_ref, g_ref, w_ref, o_ref):
    h = _rms(mem_ref[0], g_ref[...]).astype(BF16)
    o_ref[0] = _dot(h, w_ref[...]).astype(BF16)


def _mem_kv(mem, g, w_kv):
    nb = mem.shape[0]
    return pl.pallas_call(
        _memkv_kernel,
        out_shape=jax.ShapeDtypeStruct((nb, N_MEM, 2 * MEM_DIM), BF16),
        grid=(nb,),
        in_specs=[
            pl.BlockSpec((1, N_MEM, D_MODEL), lambda b: (b, 0, 0)),
            pl.BlockSpec((1, D_MODEL), lambda b: (0, 0)),
            pl.BlockSpec((D_MODEL, 2 * MEM_DIM), lambda b: (0, 0)),
        ],
        out_specs=pl.BlockSpec((1, N_MEM, 2 * MEM_DIM), lambda b: (b, 0, 0)),
        compiler_params=_cparams(("arbitrary",)),
        name="mem_kv",
    )(mem, g, w_kv)


def _inproj_kernel(x_ref, g_ref, w_ref, o_ref):
    h = _rms(x_ref[...], g_ref[...]).astype(BF16)
    for c0, cw in PROJ_CHUNKS:
        o_ref[:, c0:c0 + cw] = _dot(h, w_ref[:, c0:c0 + cw])


def _inproj(x2, g, w_in_r, tm):
    t = x2.shape[0]
    return pl.pallas_call(
        _inproj_kernel,
        out_shape=jax.ShapeDtypeStruct((t, PROJ_DIM), F32),
        grid=(t // tm,),
        in_specs=[
            pl.BlockSpec((tm, D_MODEL), lambda i: (i, 0)),
            pl.BlockSpec((1, D_MODEL), lambda i: (0, 0)),
            pl.BlockSpec((D_MODEL, PROJ_DIM), lambda i: (0, 0), pipeline_mode=pl.Buffered(1)),
        ],
        out_specs=pl.BlockSpec((tm, PROJ_DIM), lambda i: (i, 0)),
        compiler_params=_cparams(("arbitrary",)),
        name="inproj",
    )(x2, g, w_in_r)


def _gla_kernel(qf, kf, vf, lf, qb, kb, vb, lb, w2f_hi, w2f_lo, bf_ref, w2b_hi, w2b_lo, bb_ref,
                of_ref, ob_ref, sf_ref, sb_ref, *, nchunk):
    @pl.when(pl.program_id(1) == 0)
    def _():
        sf_ref[...] = jnp.zeros_like(sf_ref)
        sb_ref[...] = jnp.zeros_like(sb_ref)

    c = GLA_CHUNK
    pr = 2 * c
    dirs = (0, 1)
    heads = range(GLA_HEADS)
    ks = [slice(h * GLA_DK, (h + 1) * GLA_DK) for h in heads]
    vs = [slice(h * GLA_DV, (h + 1) * GLA_DV) for h in heads]
    pairs = [slice(p * pr, (p + 1) * pr) for p in range(nchunk // 2)]
    chunks = [slice(j * c, (j + 1) * c) for j in range(nchunk)]
    q = (qf[...], qb[...])
    k = (kf[...], kb[...])
    v16 = (vf[...].astype(BF16), vb[...].astype(BF16))
    st_refs = (sf_ref, sb_ref)

    lr = [_split2(l[...]) for l in (lf, lb)]
    w2 = ((w2f_hi[...], w2f_lo[...]), (w2b_hi[...], w2b_lo[...]))
    bias = (bf_ref[...], bb_ref[...])
    z = [_dot3(lr[d][0], lr[d][1], w2[d][0], w2[d][1]) + bias[d] for d in dirs]
    log_a = [(jnp.minimum(z[d], 0.0) - jnp.log(1.0 + jnp.exp(-jnp.abs(z[d])))) * (1.0 / GLA_GATE_TAU)
             for d in dirs]

    row = lax.broadcasted_iota(I32, (c, c), 0)
    col = lax.broadcasted_iota(I32, (c, c), 1)
    tri = (jnp.where(col <= row, 1.0, 0.0).astype(BF16), jnp.where(col >= row, 1.0, 0.0).astype(BF16))
    la = [_split2(log_a[d]) for d in dirs]
    bcum_c = [[_dot(tri[d], la[d][0][r]) + _dot(tri[d], la[d][1][r]) for r in chunks] for d in dirs]
    blast_c = [[bc[c - 1:c, :] if d == 0 else bc[0:1, :] for bc in bcum_c[d]] for d in dirs]
    bcum = [jnp.concatenate(bcum_c[d], axis=0) for d in dirs]
    blast = [jnp.concatenate([jnp.broadcast_to(b, (c, GLA_QK)) for b in blast_c[d]], axis=0) for d in dirs]
    q_s = [(q[d] * jnp.exp(bcum[d])).astype(BF16) for d in dirs]
    k_s = [(k[d] * jnp.exp(-bcum[d])).astype(BF16) for d in dirs]
    k_end = [(k[d] * jnp.exp(blast[d] - bcum[d])).astype(BF16) for d in dirs]

    prow = lax.broadcasted_iota(I32, (pr, pr), 0)
    pcol = lax.broadcasted_iota(I32, (pr, pr), 1)
    in_chunk = prow & (c - 1)
    keep = ((prow - pcol).astype(jnp.uint32) <= in_chunk.astype(jnp.uint32),
            (pcol - prow).astype(jnp.uint32) <= (c - 1 - in_chunk).astype(jnp.uint32))
    attn = [[[jnp.where(keep[d], _dot_nt(q_s[d][r, ks[h]], k_s[d][r, ks[h]]), 0.0).astype(BF16)
              for r in pairs] for h in heads] for d in dirs]
    kv = [[[_dot_tn(v16[d][r, vs[h]], k_end[d][r, ks[h]]) for r in chunks] for h in heads]
          for d in dirs]
    intra = [[jnp.concatenate([_dot(attn[d][h][p], v16[d][r, vs[h]]) for p, r in enumerate(pairs)], axis=0)
              for h in heads] for d in dirs]

    st = [[st_refs[d][h] for h in heads] for d in dirs]
    inter = [[[None] * nchunk for _ in heads] for _ in dirs]
    for step in range(nchunk):
        for d in dirs:
            j = step if d == 0 else nchunk - 1 - step
            dec = jnp.exp(blast_c[d][j])
            for h in heads:
                inter[d][h][j] = _dot_nt(q_s[d][chunks[j], ks[h]], st[d][h].astype(BF16))
                st[d][h] = st[d][h] * dec[:, ks[h]] + kv[d][h][j]
    for d, o_ref in zip(dirs, (of_ref, ob_ref)):
        for h in heads:
            st_refs[d][h] = st[d][h]
        o_ref[...] = jnp.concatenate([intra[d][h] + jnp.concatenate(inter[d][h], axis=0) for h in heads],
                                     axis=1)


def _gla(proj, w2f, bf, w2b, bb, nbatch, seq, tb):
    t = proj.shape[0]
    ns = seq // tb
    w_q, w_v = GLA_QK, GLA_V

    def fwd(cb):
        return lambda b, i: (b * ns + i, cb)

    def bwd(cb):
        return lambda b, i: (b * ns + ns - 1 - i, cb)

    const = lambda b, i: (0, 0)
    in_specs = [
        pl.BlockSpec((tb, w_q), fwd(COL_Q // w_q)),
        pl.BlockSpec((tb, w_q), fwd(COL_K // w_q)),
        pl.BlockSpec((tb, w_v), fwd(COL_V // w_v)),
        pl.BlockSpec((tb, LR_WIDTH), fwd(COL_LR // LR_WIDTH)),
        pl.BlockSpec((tb, w_q), bwd(COL_Q // w_q)),
        pl.BlockSpec((tb, w_q), bwd(COL_K // w_q)),
        pl.BlockSpec((tb, w_v), bwd(COL_V // w_v)),
        pl.BlockSpec((tb, LR_WIDTH), bwd(COL_LR // LR_WIDTH)),
        pl.BlockSpec((LR_WIDTH, GLA_QK), const),
        pl.BlockSpec((LR_WIDTH, GLA_QK), const),
        pl.BlockSpec((1, GLA_QK), const),
        pl.BlockSpec((LR_WIDTH, GLA_QK), const),
        pl.BlockSpec((LR_WIDTH, GLA_QK), const),
        pl.BlockSpec((1, GLA_QK), const),
    ]
    out_specs = [
        pl.BlockSpec((tb, w_v), lambda b, i: (b * ns + i, 0)),
        pl.BlockSpec((tb, w_v), lambda b, i: (b * ns + ns - 1 - i, 0)),
    ]
    w2f_hi, w2f_lo = w2f
    w2b_hi, w2b_lo = w2b
    return pl.pallas_call(
        functools.partial(_gla_kernel, nchunk=tb // GLA_CHUNK),
        out_shape=[jax.ShapeDtypeStruct((t, w_v), F32), jax.ShapeDtypeStruct((t, w_v), F32)],
        grid=(nbatch, ns),
        in_specs=in_specs,
        out_specs=out_specs,
        scratch_shapes=[pltpu.VMEM((GLA_HEADS, GLA_DV, GLA_DK), F32),
                        pltpu.VMEM((GLA_HEADS, GLA_DV, GLA_DK), F32)],
        compiler_params=_cparams(("arbitrary", "arbitrary")),
        name="gla",
    )(proj, proj, proj, proj, proj, proj, proj, proj, w2f_hi, w2f_lo, bf, w2b_hi, w2b_lo, bb)


def _mix_kernel(x_ref, gate_ref, pool_ref, pprev_ref, pnext_ref, r_ref, qm_ref, of_ref, ob_ref,
                kv_ref, poolw_ref, pscale_ref, gnorm_ref, wup_pool_ref, wup_gla_ref, wup_mem_ref,
                wout_ref, gffn_ref, wr_hi_ref, wr_lo_ref,
                x1_ref, h2_ref, aff_ref, pp_ref, band_ref, *, tm, seq):
    s0 = (pl.program_id(0) % (seq // tm)) * tm
    hal = POOL_PAD
    win = MIX_SUB + 2 * hal

    @pl.when(pl.program_id(0) == 0)
    def _():
        t_i = lax.broadcasted_iota(I32, (MIX_SUB, win), 0)
        j_i = lax.broadcasted_iota(I32, (MIX_SUB, win), 1)
        for g, w in enumerate(POOL_WINDOWS):
            inside = (j_i - hal - t_i + w // 2).astype(jnp.uint32) < jnp.uint32(w)
            band_ref[g] = jnp.where(inside, 1.0, 0.0).astype(BF16)

    pp_ref[0:hal, :] = jnp.where(s0 > 0, pprev_ref[...], 0.0).astype(BF16)
    pp_ref[hal:hal + tm, :] = pool_ref[...].astype(BF16)
    pp_ref[hal + tm:hal + tm + hal, :] = jnp.where(s0 + tm < seq, pnext_ref[...], 0.0).astype(BF16)
    subs = [slice(r0, r0 + MIX_SUB) for r0 in range(0, tm, MIX_SUB)]
    kv = kv_ref[0]

    groups = [slice(g * POOL_GROUP_DIM, (g + 1) * POOL_GROUP_DIM) for g in range(len(POOL_WINDOWS))]
    mheads = [slice(h * MEM_HEAD_DIM, (h + 1) * MEM_HEAD_DIM) for h in range(MEM_HEADS)]

    tots = [[_dot(band_ref[g], pp_ref[rs.start:rs.start + win, cs]) for g, cs in enumerate(groups)] for rs in subs]
    qms = [qm_ref[rs, :].astype(BF16) for rs in subs]
    scores = [[_dot_nt(qm[:, hs], kv[:, hs]) * (MEM_HEAD_DIM ** -0.5) for hs in mheads] for qm in qms]

    def pool_maps(rs, tot):
        pos = s0 + rs.start + lax.broadcasted_iota(I32, (MIX_SUB, 1), 0)
        out = []
        for g, w in enumerate(POOL_WINDOWS):
            cnt = (jnp.minimum(pos + (w - w // 2), seq) - jnp.maximum(pos - w // 2, 0)).astype(F32)
            dlt = (tot[g] / cnt - pool_ref[rs, groups[g]]).astype(BF16)
            out.append(_dot(dlt, poolw_ref[g]))
        return (jnp.concatenate(out, axis=1) * pscale_ref[...]).astype(BF16)

    def mem_values(sc):
        om = []
        for h, s in enumerate(sc):
            e = jnp.exp(s - jnp.max(s, axis=-1, keepdims=True))
            p = (e / jnp.sum(e, axis=-1, keepdims=True)).astype(BF16)
            om.append(_dot(p, kv[:, MEM_DIM + h * MEM_HEAD_DIM:MEM_DIM + (h + 1) * MEM_HEAD_DIM]))
        return jnp.concatenate(om, axis=1).astype(BF16)

    def gla_out(rs):
        o = of_ref[rs, :] + ob_ref[rs, :]
        on = []
        for h in range(GLA_HEADS):
            oh = o[:, h * GLA_DV:(h + 1) * GLA_DV]
            on.append(oh * lax.rsqrt(jnp.mean(oh * oh, axis=-1, keepdims=True) + EPS))
        r = r_ref[rs, :]
        silu = r + jnp.tanh(r) * r
        return (jnp.concatenate(on, axis=1) * gnorm_ref[...] * silu).astype(BF16)

    y_pool = [pool_maps(rs, tot) for rs, tot in zip(subs, tots)]
    y_mem = [mem_values(sc) for sc in scores]
    y_gla = [gla_out(rs) for rs in subs]

    ups = [(_dot(yp, wup_pool_ref[...]), _dot(yg, wup_gla_ref[...]), _dot(ym, wup_mem_ref[...]))
           for yp, yg, ym in zip(y_pool, y_gla, y_mem)]

    def merge(rs, ups):
        gated = [y + jnp.tanh(gate_ref[rs, j * D_MODEL:(j + 1) * D_MODEL]) * y for j, y in enumerate(ups)]
        merged = (gated[0] + gated[1] + gated[2]).astype(BF16)
        x1 = x_ref[rs, :] + _dot(merged, wout_ref[...])
        x1_ref[rs, :] = x1
        return x1

    def route(rs, x1):
        h2 = _rms(x1, gffn_ref[...])
        h_hi, h_lo = _split2(h2)
        h2_ref[rs, :] = h_hi
        wr_hi, wr_lo = wr_hi_ref[...], wr_lo_ref[...]
        logits = _dot_nt(wr_hi, h_hi) + _dot_nt(wr_hi, h_lo) + _dot_nt(wr_lo, h_hi)
        e = jnp.exp(logits - jnp.max(logits, axis=0, keepdims=True))
        aff_ref[:, rs] = e / jnp.sum(e, axis=0, keepdims=True)

    x1s = [merge(rs, u) for rs, u in zip(subs, ups)]
    for rs, x1 in zip(subs, x1s):
        route(rs, x1)


def _mix(x2, proj, o_f, o_b, kv, wts, nbatch, seq, tm):
    t = x2.shape[0]
    ns = seq // tm
    h8 = tm // POOL_PAD
    n8 = t // POOL_PAD
    const2 = lambda i: (0, 0)
    in_specs = [
        pl.BlockSpec((tm, D_MODEL), lambda i: (i, 0)),
        pl.BlockSpec((tm, N_BRANCH * D_MODEL), lambda i: (i, COL_GATE // (N_BRANCH * D_MODEL))),
        pl.BlockSpec((tm, POOL_DIM), lambda i: (i, COL_POOL // POOL_DIM)),
        pl.BlockSpec((POOL_PAD, POOL_DIM), lambda i: (jnp.maximum(i * h8 - 1, 0), COL_POOL // POOL_DIM)),
        pl.BlockSpec((POOL_PAD, POOL_DIM), lambda i: (jnp.minimum((i + 1) * h8, n8 - 1), COL_POOL // POOL_DIM)),
        pl.BlockSpec((tm, GLA_V), lambda i: (i, COL_R // GLA_V)),
        pl.BlockSpec((tm, MEM_DIM), lambda i: (i, COL_QM // MEM_DIM)),
        pl.BlockSpec((tm, GLA_V), lambda i: (i, 0)),
        pl.BlockSpec((tm, GLA_V), lambda i: (i, 0)),
        pl.BlockSpec((1, N_MEM, 2 * MEM_DIM), lambda i: (i // ns, 0, 0)),
        pl.BlockSpec((len(POOL_WINDOWS), POOL_GROUP_DIM, POOL_GROUP_DIM), lambda i: (0, 0, 0)),
        pl.BlockSpec((1, POOL_DIM), const2),
        pl.BlockSpec((1, GLA_V), const2),
        pl.BlockSpec((POOL_DIM, D_MODEL), const2),
        pl.BlockSpec((GLA_V, D_MODEL), const2),
        pl.BlockSpec((MEM_DIM, D_MODEL), const2),
        pl.BlockSpec((D_MODEL, D_MODEL), const2),
        pl.BlockSpec((1, D_MODEL), const2),
        pl.BlockSpec((N_EXPERTS, D_MODEL), const2),
        pl.BlockSpec((N_EXPERTS, D_MODEL), const2),
    ]
    out_specs = [
        pl.BlockSpec((tm, D_MODEL), lambda i: (i, 0)),
        pl.BlockSpec((tm, D_MODEL), lambda i: (i, 0)),
        pl.BlockSpec((N_EXPERTS, tm), lambda i: (0, i)),
    ]
    return pl.pallas_call(
        functools.partial(_mix_kernel, tm=tm, seq=seq),
        out_shape=[jax.ShapeDtypeStruct((t, D_MODEL), F32), jax.ShapeDtypeStruct((t, D_MODEL), BF16),
                   jax.ShapeDtypeStruct((N_EXPERTS, t), F32)],
        grid=(t // tm,),
        in_specs=in_specs,
        out_specs=out_specs,
        scratch_shapes=[pltpu.VMEM((tm + 2 * POOL_PAD, POOL_DIM), BF16),
                        pltpu.VMEM((len(POOL_WINDOWS), MIX_SUB, MIX_SUB + 2 * POOL_PAD), BF16)],
        compiler_params=_cparams(("arbitrary",)),
        name="mix",
    )(x2, proj, proj, proj, proj, proj, proj, o_f, o_b, kv, *wts)


def _route_kernel(abt_ref, rank_ref, slo_ref, cnt_ref, thr_ref, *, nt, td, cap):
    ne = N_EXPERTS
    aff_all = abt_ref[...]

    def bit_step(i, thr):
        cand = thr | (jnp.int32(1) << (29 - i))
        hit = jnp.where(aff_all >= lax.bitcast_convert_type(cand, F32), 1.0, 0.0)
        cnt = jnp.sum(jnp.sum(hit, axis=1, keepdims=True), axis=2, keepdims=True)
        return jnp.where(cnt >= cap, cand, thr)

    thr = lax.fori_loop(0, 30, bit_step, jnp.zeros((ne, 1, 1), I32))
    thr_ref[...] = jnp.broadcast_to(lax.bitcast_convert_type(thr, F32), thr_ref.shape)

    t_row = lax.broadcasted_iota(I32, (td, td), 0)
    t_col = lax.broadcasted_iota(I32, (td, td), 1)
    upto = jnp.where(t_row <= t_col, 1.0, 0.0).astype(BF16)
    ones_t = jnp.ones((td, LANES), BF16)
    i_row = lax.broadcasted_iota(I32, (nt, nt), 0)
    i_col = lax.broadcasted_iota(I32, (nt, nt), 1)
    before = jnp.where(i_col < i_row, 1.0, 0.0).astype(BF16)

    def expert(e, carry):
        thr_e = thr_ref[e][0:1, :]
        aff = abt_ref[e]
        gt = jnp.where(aff > thr_e, 1.0, 0.0)
        eq = jnp.where(aff == thr_e, 1.0, 0.0)
        n_gt = jnp.sum(jnp.sum(gt, axis=0, keepdims=True), axis=1, keepdims=True)
        need = cap - n_gt
        eq16 = eq.astype(BF16)
        tot_eq = _dot(eq16, ones_t)
        ex_eq = _dot(before, tot_eq.astype(BF16))
        rel_eq = _dot(eq16, upto)
        sel = gt + eq * jnp.where(ex_eq[:, 0:1] + rel_eq <= need, 1.0, 0.0)
        rank_ref[e] = jnp.where(sel > 0.0, _dot(sel.astype(BF16), upto) - 1.0, -1.0)
        cnt = _dot(sel.astype(BF16), ones_t)
        pad = jnp.floor((cnt + float(ROW_ALIGN - 1)) * (1.0 / ROW_ALIGN)) * float(ROW_ALIGN)
        slo_ref[e] = _dot(before, pad.astype(BF16)).astype(I32)
        cnt_ref[e] = cnt.astype(I32)
        return carry

    lax.fori_loop(0, ne, expert, 0)


def _route(a_bt, cap):
    ne, nt, td = a_bt.shape
    return pl.pallas_call(
        functools.partial(_route_kernel, nt=nt, td=td, cap=cap),
        out_shape=[jax.ShapeDtypeStruct((ne, nt, td), F32), jax.ShapeDtypeStruct((ne, nt, LANES), I32),
                   jax.ShapeDtypeStruct((ne, nt, LANES), I32)],
        grid=(1,),
        in_specs=[pl.BlockSpec((ne, nt, td), lambda i: (0, 0, 0))],
        out_specs=[pl.BlockSpec((ne, nt, td), lambda i: (0, 0, 0)), pl.BlockSpec((ne, nt, LANES), lambda i: (0, 0, 0)),
                   pl.BlockSpec((ne, nt, LANES), lambda i: (0, 0, 0))],
        scratch_shapes=[pltpu.VMEM((ne, 8, td), F32)],
        compiler_params=_cparams(("arbitrary",)),
        name="route",
    )(a_bt)


def _segment_blocks(rank, chunk):
    td = rank.shape[1]
    r_iota = lax.broadcasted_iota(I32, (SEG_ROWS, td), 0).astype(F32) + jnp.asarray(chunk * SEG_ROWS, F32)
    return [jnp.where(rank[e:e + 1, :] == r_iota, 1.0, 0.0) for e in range(N_EXPERTS)]


def _segment_onehot(rank, chunk):
    return jnp.concatenate(_segment_blocks(rank, chunk), axis=0).astype(BF16)


def _dispatch_kernel(slo_ref, cnt_ref, vend_ref, rank_ref, h2_ref, aff_ref, xe_hbm, ge_hbm,
                     xbuf, gbuf, xov, gov, sem_x, sem_g, sem_ov, *, nt, cpad):
    i = pl.program_id(0)
    slot = i % 2

    @pl.when(i == 0)
    def _():
        xov[...] = jnp.zeros_like(xov)
        gov[...] = jnp.zeros_like(gov)

        def fill(e, first, rows, wait):
            row0 = pl.multiple_of(first, ROW_ALIGN)
            cx = pltpu.make_async_copy(xov.at[pl.ds(0, rows), :], xe_hbm.at[pl.ds(row0, rows), :], sem_ov)
            cg = pltpu.make_async_copy(gov.at[pl.ds(0, rows), :], ge_hbm.at[pl.ds(row0, rows), :], sem_ov)
            if wait:
                cx.wait()
                cg.wait()
            else:
                cx.start()
                cg.start()

        for wait in (False, True):
            for e in range(N_EXPERTS):
                tail0 = e * cpad + vend_ref[e]
                n_big = (cpad - vend_ref[e]) // SEG_ROWS
                n_small = ((cpad - vend_ref[e]) % SEG_ROWS) // ROW_ALIGN

                def big(k, c, tail0=tail0, e=e, wait=wait):
                    fill(e, tail0 + k * SEG_ROWS, SEG_ROWS, wait)
                    return c

                def small(k, c, tail0=tail0, n_big=n_big, e=e, wait=wait):
                    fill(e, tail0 + n_big * SEG_ROWS + k * ROW_ALIGN, ROW_ALIGN, wait)
                    return c

                lax.fori_loop(0, n_big, big, 0)
                lax.fori_loop(0, n_small, small, 0)

    def seg_copies(step, buf_slot, e):
        row0 = pl.multiple_of(slo_ref[step, e], ROW_ALIGN) + e * cpad
        src = pl.ds(e * SEG_ROWS, SEG_ROWS)
        return (pltpu.make_async_copy(xbuf.at[buf_slot, src, :], xe_hbm.at[pl.ds(row0, SEG_ROWS), :], sem_x),
                pltpu.make_async_copy(gbuf.at[buf_slot, src, :], ge_hbm.at[pl.ds(row0, SEG_ROWS), :], sem_g))

    def wait_step(step, buf_slot):
        for e in range(N_EXPERTS):
            cx, cg = seg_copies(step, buf_slot, e)
            cx.wait()
            cg.wait()

    rank = rank_ref[...]
    h2 = h2_ref[...]
    aff = aff_ref[...]

    def rows_for(chunk):
        blocks = _segment_blocks(rank, chunk)
        xr = _dot(jnp.concatenate(blocks, axis=0).astype(BF16), h2).astype(BF16)
        gs = [jnp.broadcast_to(jnp.sum(blocks[e] * aff[e:e + 1, :], axis=1, keepdims=True), (SEG_ROWS, LANES))
              for e in range(N_EXPERTS)]
        return xr, jnp.concatenate(gs, axis=0)

    xr, gr = rows_for(0)
    xbuf[slot] = xr
    gbuf[slot] = gr

    @pl.when(i > 0)
    def _():
        wait_step(i - 1, 1 - slot)

    for e in range(N_EXPERTS):
        cx, cg = seg_copies(i, slot, e)
        cx.start()
        cg.start()

    cmax = cnt_ref[i, 0]
    for e in range(1, N_EXPERTS):
        cmax = jnp.maximum(cmax, cnt_ref[i, e])

    def extra_chunk(chunk, carry):
        xo, go = rows_for(chunk)
        xov[...] = xo
        gov[...] = go
        for e in range(N_EXPERTS):
            @pl.when(cnt_ref[i, e] > chunk * SEG_ROWS)
            def _(e=e):
                row0 = pl.multiple_of(slo_ref[i, e] + chunk * SEG_ROWS, ROW_ALIGN) + e * cpad
                src = pl.ds(e * SEG_ROWS, SEG_ROWS)
                cx = pltpu.make_async_copy(xov.at[src, :], xe_hbm.at[pl.ds(row0, SEG_ROWS), :], sem_ov)
                cg = pltpu.make_async_copy(gov.at[src, :], ge_hbm.at[pl.ds(row0, SEG_ROWS), :], sem_ov)
                cx.start()
                cg.start()
                cx.wait()
                cg.wait()
        return carry

    lax.fori_loop(1, (cmax + SEG_ROWS - 1) // SEG_ROWS, extra_chunk, 0)

    @pl.when(i == nt - 1)
    def _():
        wait_step(i, slot)


def _dispatch(slo_t, cnt_t, vend, rank2, h2, aff, cpad):
    ne, t = rank2.shape
    td = DISPATCH_TILE
    nt = t // td
    grid_spec = pltpu.PrefetchScalarGridSpec(
        num_scalar_prefetch=3,
        grid=(nt,),
        in_specs=[
            pl.BlockSpec((ne, td), lambda i, s, c, v: (0, i)),
            pl.BlockSpec((td, D_MODEL), lambda i, s, c, v: (i, 0)),
            pl.BlockSpec((ne, td), lambda i, s, c, v: (0, i)),
        ],
        out_specs=[pl.BlockSpec(memory_space=pl.ANY), pl.BlockSpec(memory_space=pl.ANY)],
        scratch_shapes=[pltpu.VMEM((2, ne * SEG_ROWS, D_MODEL), BF16), pltpu.VMEM((2, ne * SEG_ROWS, LANES), F32),
                        pltpu.VMEM((ne * SEG_ROWS, D_MODEL), BF16), pltpu.VMEM((ne * SEG_ROWS, LANES), F32),
                        pltpu.SemaphoreType.DMA, pltpu.SemaphoreType.DMA, pltpu.SemaphoreType.DMA],
    )
    return pl.pallas_call(
        functools.partial(_dispatch_kernel, nt=nt, cpad=cpad),
        out_shape=[jax.ShapeDtypeStruct((ne * cpad, D_MODEL), BF16), jax.ShapeDtypeStruct((ne * cpad, LANES), F32)],
        grid_spec=grid_spec,
        compiler_params=_cparams(("arbitrary",)),
        name="dispatch",
    )(slo_t, cnt_t, vend, rank2, h2, aff)


def _ffn_kernel(vend_ref, x_ref, g_ref, wg_ref, wu_ref, wd_ref, o_ref, *, ft, nj):
    e = pl.program_id(0)
    row0 = (nj - 1 - pl.program_id(1)) * ft
    vend = vend_ref[e]

    @pl.when(row0 < vend)
    def _():
        valid = row0 + lax.broadcasted_iota(I32, (ft, 1), 0) < vend
        x = jnp.where(valid, x_ref[...], jnp.zeros((), BF16))
        hg = _dot(x, wg_ref[0])
        hu = _dot(x, wu_ref[0])
        hid = ((hg + jnp.tanh(hg) * hg) * hu).astype(BF16)
        gate = jnp.where(valid, g_ref[:, 0:1], 0.0)
        o_ref[...] = (_dot(hid, wd_ref[0]) * gate).astype(BF16)

    @pl.when(row0 >= vend)
    def _():
        o_ref[...] = jnp.zeros_like(o_ref)


def _ffn(vend, xe, ge, w_gate, w_up, w_down, cpad, ft):
    ne = N_EXPERTS
    nj = cpad // ft
    wspec = pl.BlockSpec((1, D_MODEL, D_MODEL), lambda e, j, v: (e, 0, 0))

    def in_tile(e, j, v):
        return jnp.minimum(nj - 1 - j, (v[e] - 1) // ft)

    grid_spec = pltpu.PrefetchScalarGridSpec(
        num_scalar_prefetch=1,
        grid=(ne, nj),
        in_specs=[
            pl.BlockSpec((ft, D_MODEL), lambda e, j, v: (e * nj + in_tile(e, j, v), 0)),
            pl.BlockSpec((ft, LANES), lambda e, j, v: (e * nj + in_tile(e, j, v), 0)),
            wspec, wspec, wspec,
        ],
        out_specs=pl.BlockSpec((ft, D_MODEL), lambda e, j, v: (e * nj + nj - 1 - j, 0)),
    )
    return pl.pallas_call(
        functools.partial(_ffn_kernel, ft=ft, nj=nj),
        out_shape=jax.ShapeDtypeStruct((ne * cpad, D_MODEL), BF16),
        grid_spec=grid_spec,
        compiler_params=_cparams(("arbitrary", "arbitrary")),
        name="ffn",
    )(vend, xe, ge, w_gate, w_up, w_down)


def _combine_kernel(slo_ref, cnt_ref, rank_ref, x1_ref, gfin_ref, ye_hbm, y_ref, ybuf, yov, acc_ref,
                    sem, sem_ov, *, nsteps, subs, cpad):
    i = pl.program_id(0)
    slot = i % 2
    td = DISPATCH_TILE

    def seg_copy(step, buf_slot, u, e):
        row0 = pl.multiple_of(slo_ref[step * subs + u, e], ROW_ALIGN) + e * cpad
        return pltpu.make_async_copy(ye_hbm.at[pl.ds(row0, SEG_ROWS), :],
                                     ybuf.at[buf_slot, u, pl.ds(e * SEG_ROWS, SEG_ROWS), :], sem.at[buf_slot])

    def fetch(step, buf_slot):
        for u in range(subs):
            for e in range(N_EXPERTS):
                seg_copy(step, buf_slot, u, e).start()

    @pl.when(i == 0)
    def _():
        fetch(0, 0)

    @pl.when(i + 1 < nsteps)
    def _():
        fetch(i + 1, 1 - slot)

    for u in range(subs):
        for e in range(N_EXPERTS):
            seg_copy(i, slot, u, e).wait()

    ranks = [rank_ref[:, u * td:(u + 1) * td] for u in range(subs)]
    onehots = [_segment_onehot(r, 0) for r in ranks]
    rows = [ybuf[slot, u] for u in range(subs)]
    for u in range(subs):
        acc_ref[u * td:(u + 1) * td, :] = x1_ref[u * td:(u + 1) * td, :] + _dot_tn(onehots[u], rows[u])

    for u in range(subs):
        tile = i * subs + u
        cmax = cnt_ref[tile, 0]
        for e in range(1, N_EXPERTS):
            cmax = jnp.maximum(cmax, cnt_ref[tile, e])

        def extra_chunk(chunk, carry, u=u, tile=tile):
            for e in range(N_EXPERTS):
                dst = pl.ds(e * SEG_ROWS, SEG_ROWS)

                @pl.when(cnt_ref[tile, e] > chunk * SEG_ROWS)
                def _(e=e, dst=dst):
                    row0 = pl.multiple_of(slo_ref[tile, e] + chunk * SEG_ROWS, ROW_ALIGN) + e * cpad
                    cp = pltpu.make_async_copy(ye_hbm.at[pl.ds(row0, SEG_ROWS), :], yov.at[dst, :], sem_ov)
                    cp.start()
                    cp.wait()

                @pl.when(cnt_ref[tile, e] <= chunk * SEG_ROWS)
                def _(dst=dst):
                    yov[dst, :] = jnp.zeros((SEG_ROWS, D_MODEL), BF16)
            acc_ref[u * td:(u + 1) * td, :] += _dot_tn(_segment_onehot(ranks[u], chunk), yov[...])
            return carry

        lax.fori_loop(1, (cmax + SEG_ROWS - 1) // SEG_ROWS, extra_chunk, 0)

    y_ref[...] = _rms(acc_ref[...], gfin_ref[...])


def _combine(slo_t, cnt_t, rank2, x1, g_final, ye, cpad):
    ne, t = rank2.shape
    nt = t // DISPATCH_TILE
    subs = _pick(nt, (4, 2, 1))
    tc = subs * DISPATCH_TILE
    grid_spec = pltpu.PrefetchScalarGridSpec(
        num_scalar_prefetch=2,
        grid=(nt // subs,),
        in_specs=[
            pl.BlockSpec((ne, tc), lambda i, s, c: (0, i)),
            pl.BlockSpec((tc, D_MODEL), lambda i, s, c: (i, 0)),
            pl.BlockSpec((1, D_MODEL), lambda i, s, c: (0, 0)),
            pl.BlockSpec(memory_space=pl.ANY),
        ],
        out_specs=pl.BlockSpec((tc, D_MODEL), lambda i, s, c: (i, 0)),
        scratch_shapes=[pltpu.VMEM((2, subs, ne * SEG_ROWS, D_MODEL), BF16),
                        pltpu.VMEM((ne * SEG_ROWS, D_MODEL), BF16),
                        pltpu.VMEM((tc, D_MODEL), F32),
                        pltpu.SemaphoreType.DMA((2,)), pltpu.SemaphoreType.DMA],
    )
    return pl.pallas_call(
        functools.partial(_combine_kernel, nsteps=nt // subs, subs=subs, cpad=cpad),
        out_shape=jax.ShapeDtypeStruct((t, D_MODEL), F32),
        grid_spec=grid_spec,
        compiler_params=_cparams(("arbitrary",)),
        name="combine",
    )(slo_t, cnt_t, rank2, x1, g_final, ye)


def _prep_weights(norm_mix_g, w_in, gla_w2_f, gla_b_f, gla_w2_b, gla_b_b, gla_norm_g, pool_w, pool_scale,
                  mem_norm_g, w_mem_kv, w_up_pool, w_up_gla, w_up_mem, w_out, norm_ffn_g, w_router,
                  w_e_gate, w_e_up, w_e_down, norm_final_g):
    o_pool, o_q, o_k, o_v, o_r = 0, 512, 768, 1024, 1536
    o_lf, o_lb, o_qm, o_gate, o_end = 2048, 2064, 2080, 2592, 5664
    w = w_in[0]
    pad = jnp.zeros((D_MODEL, LR_WIDTH - 2 * GLA_GATE_RANK), F32)
    w_in_r = jnp.concatenate([w[:, o_gate:o_end] * 0.5, w[:, o_pool:o_q], w[:, o_v:o_r], w[:, o_r:o_lf] * 0.5,
                              w[:, o_qm:o_gate], w[:, o_q:o_k] * (GLA_DK ** -0.5), w[:, o_k:o_v], w[:, o_lf:o_lb],
                              w[:, o_lb:o_qm], pad], axis=1).astype(BF16)

    def pad_w2(w2, row0):
        full = jnp.zeros((LR_WIDTH, GLA_QK), F32).at[row0:row0 + GLA_GATE_RANK].set(w2)
        return _split2(full)

    row = lambda v: v.reshape(1, -1)
    wr_hi, wr_lo = _split2(w_router[0].T)
    return dict(
        norm_mix_g=row(norm_mix_g[0]), w_in_r=w_in_r,
        w2f=pad_w2(gla_w2_f[0], 0), bf=row(gla_b_f[0]),
        w2b=pad_w2(gla_w2_b[0], GLA_GATE_RANK), bb=row(gla_b_b[0]),
        mem_norm_g=row(mem_norm_g[0]), w_mem_kv=w_mem_kv[0].astype(BF16),
        mix=(pool_w[0].astype(BF16), row(pool_scale[0]), row(gla_norm_g[0]), w_up_pool[0].astype(BF16),
             w_up_gla[0].astype(BF16), w_up_mem[0].astype(BF16), (w_out[0] * 0.5).astype(BF16),
             row(norm_ffn_g[0]), wr_hi, wr_lo),
        w_e_gate=(w_e_gate[0] * 0.5).astype(BF16), w_e_up=w_e_up[0].astype(BF16), w_e_down=w_e_down[0].astype(BF16),
        norm_final_g=row(norm_final_g),
    )


def _pick(n, pref):
    for c in pref:
        if n % c == 0:
            return c
    raise ValueError(f"no tile for {n}")


def _trunk(x, mem, p):
    nbatch, seq, _ = x.shape
    t = nbatch * seq
    x2 = x.reshape(t, D_MODEL)
    tm = _pick(seq, (512, 256, 128))
    tb = _pick(seq, (512, 256, 128))
    kv = _mem_kv(mem, p["mem_norm_g"], p["w_mem_kv"])
    proj = _inproj(x2, p["norm_mix_g"], p["w_in_r"], tm)
    o_f, o_b = _gla(proj, p["w2f"], p["bf"], p["w2b"], p["bb"], nbatch, seq, tb)
    x1, h2, aff = _mix(x2, proj, o_f, o_b, kv, p["mix"], nbatch, seq, _pick(seq, (4 * MIX_SUB, MIX_SUB)))
    cap = max(1, min(t, EC_CAPACITY_FACTOR * t // N_EXPERTS))
    nt = t // DISPATCH_TILE
    rank, slo, cnt = _route(aff.reshape(N_EXPERTS, nt, DISPATCH_TILE), cap)
    rank2 = rank.reshape(N_EXPERTS, t)
    slo_t = slo[:, :, 0].T
    cnt_t = cnt[:, :, 0].T
    last_chunks = jnp.maximum(-(-cnt_t[nt - 1] // SEG_ROWS), 1)
    vend = slo_t[nt - 1] + last_chunks * SEG_ROWS
    cpad = -(-(cap + (ROW_ALIGN - 1) * nt + SEG_ROWS) // FFN_TILE) * FFN_TILE
    xe, ge = _dispatch(slo_t, cnt_t, vend, rank2, h2, aff, cpad)
    ye = _ffn(vend, xe, ge, p["w_e_gate"], p["w_e_up"], p["w_e_down"], cpad, FFN_TILE)
    y = _combine(slo_t, cnt_t, rank2, x1, p["norm_final_g"], ye, cpad)
    return y.reshape(nbatch, seq, D_MODEL)


def kernel(x_prompt, x_sample, mem_prompt, mem_sample, norm_mix_g, w_in, gla_w2_f, gla_b_f, gla_w2_b, gla_b_b, gla_norm_g, pool_w, pool_scale, mem_norm_g, w_mem_kv, w_up_pool, w_up_gla, w_up_mem, w_out, norm_ffn_g, w_router, w_e_gate, w_e_up, w_e_down, norm_final_g):
    p = _prep_weights(norm_mix_g, w_in, gla_w2_f, gla_b_f, gla_w2_b, gla_b_b, gla_norm_g, pool_w, pool_scale,
                      mem_norm_g, w_mem_kv, w_up_pool, w_up_gla, w_up_mem, w_out, norm_ffn_g, w_router,
                      w_e_gate, w_e_up, w_e_down, norm_final_g)
    return (_trunk(x_prompt, mem_prompt, p), _trunk(x_sample, mem_sample, p))
```
